```python
import math, functools
import jax, jax.numpy as jnp
from jax import lax
import numpy as np

D_MODEL = 1024
BATCH = 4
SEQ = 4096
DEPTH = 1
DEC_BATCH = 32
DEC_SEQ = 1
PAST_LEN = 8192
PAGE_SIZE = 128

N_META = 16
RW_HEADS = 8
RW_DH = 64
RW_W = RW_HEADS * RW_DH
DECAY_LORA = 64
AAA_LORA = 64
GATE_LORA = 128
RW_COLS = 3 * RW_W + DECAY_LORA + AAA_LORA + GATE_LORA
RW_SPLITS = (RW_W, 2 * RW_W, 3 * RW_W, 3 * RW_W + DECAY_LORA, 3 * RW_W + DECAY_LORA + AAA_LORA)
LNX_EPS = 64e-5
DF_HEADS = 4
DF_DK = 64
DF_DV = 2 * DF_DK
DF_W = DF_HEADS * DF_DV
DF_COLS = 3 * DF_W
MIX_W = RW_W + DF_W
IN_COLS = RW_COLS + DF_COLS
ROPE_THETA = 10000.0
Q_BLOCK = 128
SUBLN_EPS = 1e-5
LN_EPS = 1e-5
PK_HEADS = 8
N_KEYS = 128
N_EXPERTS = N_KEYS * N_KEYS
PK_DQ = 128
PK_TOPK = 16
PEER_BLOCK = 256
DEEP_ALPHA = (2 * DEPTH) ** 0.25
DEEP_BETA = (8 * DEPTH) ** -0.25

kernel_name = 'hymba_rwkv7_diffattn_peer_decode_step'


def layer_norm(x, w, b):
    xf = x.astype(jnp.float32)
    mu = xf.mean(-1, keepdims=True)
    var = jnp.mean(jnp.square(xf - mu), -1, keepdims=True)
    return ((xf - mu) * lax.rsqrt(var + LN_EPS) * w.astype(jnp.float32) + b.astype(jnp.float32)).astype(x.dtype)


def rope(x, pos):
    half = x.shape[-1] // 2
    inv = ROPE_THETA ** (-jnp.arange(half, dtype=jnp.float32) / half)
    ang = pos.astype(jnp.float32)[:, None] * inv[None, :]
    ang = ang.reshape(ang.shape[:1] + (1,) * (x.ndim - 3) + (half,))
    cos, sin = jnp.cos(ang), jnp.sin(ang)
    xf = x.astype(jnp.float32)
    x1, x2 = xf[..., :half], xf[..., half:]
    return jnp.concatenate([x1 * cos - x2 * sin, x1 * sin + x2 * cos], -1).astype(x.dtype)


def wkv_scan(S0, r, w, k, v, kk, a):
    def step(S, inp):
        r_t, w_t, k_t, v_t, kk_t, a_t = inp
        sa = jnp.einsum('bhvk,bhk->bhv', S, -kk_t)
        S = (S * w_t[:, :, None, :] + sa[..., None] * (kk_t * a_t)[:, :, None, :]
             + v_t[..., None] * k_t[:, :, None, :])
        return S, jnp.einsum('bhvk,bhk->bhv', S, r_t)
    xs = tuple(jnp.moveaxis(t, 1, 0) for t in (r, w, k, v, kk, a))
    S, ys = lax.scan(step, S0, xs)
    return S, jnp.moveaxis(ys, 0, 1)


def rwkv7_group(cols, shift0, wkv0, lw):
    B, T, _ = cols.shape
    f32 = lambda t: t.astype(jnp.float32)
    prev = jnp.concatenate([shift0[:, None].astype(cols.dtype), cols[:, :-1]], axis=1)
    mixed = f32(cols + (prev - cols) * lw['mu_shift'])
    r, k, v, xw, xa, xg = jnp.split(mixed, RW_SPLITS, axis=-1)
    log_w = -jnp.exp(-jax.nn.softplus(-(f32(lw['w0']) + jnp.tanh(xw) @ f32(lw['w2_decay']))) - 0.5)
    decay = jnp.exp(log_w)
    a = jax.nn.sigmoid(f32(lw['a0']) + xa @ f32(lw['a2']))
    g = jax.nn.sigmoid(xg) @ f32(lw['g2'])
    kk = k * f32(lw['k_k'])
    k = k * (1.0 + (a - 1.0) * f32(lw['k_a']))
    hs = lambda t: t.reshape(B, T, RW_HEADS, RW_DH)
    r, k, v, kk, a, decay = (hs(t) for t in (r, k, v, kk, a, decay))
    kk = kk / jnp.maximum(jnp.sqrt(jnp.sum(kk * kk, -1, keepdims=True)), 1e-12)
    S, y = wkv_scan(f32(wkv0), r, decay, k, v, kk, a)
    mu = y.mean(-1, keepdims=True)
    var = jnp.mean(jnp.square(y - mu), -1, keepdims=True)
    yn = ((y - mu) * lax.rsqrt(var + LNX_EPS) * f32(lw['lnx_w']).reshape(RW_HEADS, RW_DH)
          + f32(lw['lnx_b']).reshape(RW_HEADS, RW_DH))
    bonus = jnp.sum(r * k * f32(lw['r_k']), -1, keepdims=True) * v
    out = (yn + bonus).reshape(B, T, RW_W) * g
    return out.astype(cols.dtype), S


def diff_combine(q, k, v, mask, lam):
    s = jnp.einsum('bqhcd,bkhcd->bchqk', q, k).astype(jnp.float32) * (DF_DK ** -0.5)
    s = jnp.where(mask, s, -jnp.inf)
    p = jax.nn.softmax(s, axis=-1)
    att = p[:, 0] - lam.astype(jnp.float32) * p[:, 1]
    return jnp.einsum('bhqk,bkhd->bqhd', att.astype(v.dtype), v)


def diff_attn_prompt(q, k, v, lam):
    B, T, H = q.shape[:3]
    Tp = -(-T // Q_BLOCK) * Q_BLOCK
    pw = Tp - T
    qp = jnp.pad(q, ((0, 0), (0, pw), (0, 0), (0, 0), (0, 0)))
    kp = jnp.pad(k, ((0, 0), (0, pw), (0, 0), (0, 0), (0, 0)))
    vp = jnp.pad(v, ((0, 0), (0, pw), (0, 0), (0, 0)))
    nb = Tp // Q_BLOCK
    qb = jnp.moveaxis(qp.reshape(B, nb, Q_BLOCK, H, 2, DF_DK), 1, 0)
    kpos = jnp.arange(Tp)

    def block(args):
        qblk, start = args
        qpos = start + jnp.arange(Q_BLOCK)
        return diff_combine(qblk, kp, vp, kpos[None, :] <= qpos[:, None], lam)

    out = lax.map(block, (qb, jnp.arange(nb) * Q_BLOCK))
    return jnp.moveaxis(out, 0, 1).reshape(B, Tp, H, DF_DV)[:, :T]


def diff_attn_sample(q, k, v, lam, cache_k, cache_v, page_table):
    B, T = q.shape[:2]
    k_past = cache_k[page_table]
    k_past = k_past.reshape((B, -1) + k_past.shape[3:])
    v_past = cache_v[page_table].reshape(B, -1, DF_HEADS, DF_DV)
    past = k_past.shape[1]
    keys = jnp.concatenate([k_past.astype(k.dtype), k], axis=1)
    vals = jnp.concatenate([v_past.astype(v.dtype), v], axis=1)
    kpos = jnp.arange(past + T)
    qpos = past + jnp.arange(T)
    return diff_combine(q, keys, vals, kpos[None, :] <= qpos[:, None], lam)


def peer(x, wq, subkeys, U, V):
    n = x.shape[0]
    blk = min(PEER_BLOCK, n)
    n_pad = -(-n // blk) * blk
    xb = jnp.pad(x, ((0, n_pad - n), (0, 0))).reshape(n_pad // blk, blk, D_MODEL)

    def block(xt):
        q = (xt @ wq).reshape(blk, PK_HEADS, 2, PK_DQ // 2)
        s = jnp.einsum('thcd,hckd->thck', q, subkeys).astype(jnp.float32)
        s1, i1 = lax.top_k(s[:, :, 0], PK_TOPK)
        s2, i2 = lax.top_k(s[:, :, 1], PK_TOPK)
        cand_s = (s1[..., :, None] + s2[..., None, :]).reshape(blk, PK_HEADS, PK_TOPK * PK_TOPK)
        cand_i = (i1[..., :, None] * N_KEYS + i2[..., None, :]).reshape(blk, PK_HEADS, PK_TOPK * PK_TOPK)
        top_s, sel = lax.top_k(cand_s, PK_TOPK)
        idx = jnp.take_along_axis(cand_i, sel, axis=-1)
        gate = jax.nn.softmax(top_s, axis=-1)
        act = jax.nn.gelu(jnp.einsum('thed,td->the', U[idx], xt).astype(jnp.float32), approximate=False)
        return jnp.einsum('the,thed->td', (gate * act).astype(V.dtype), V[idx])

    return lax.map(block, xb).reshape(n_pad, D_MODEL)[:n]


def trunk_layer(x, pos, wkv0, shift0, attend, lw, lam_init):
    B, T, _ = x.shape
    cols = x @ lw['w_in']
    rw_cols, df_cols = cols[..., :RW_COLS], cols[..., RW_COLS:]
    rw_out, S = rwkv7_group(rw_cols, shift0, wkv0, lw)
    q = rope(df_cols[..., :DF_W].reshape(B, T, DF_HEADS, 2, DF_DK), pos)
    k = rope(df_cols[..., DF_W:2 * DF_W].reshape(B, T, DF_HEADS, 2, DF_DK), pos)
    v = df_cols[..., 2 * DF_W:].reshape(B, T, DF_HEADS, DF_DV)
    f32 = lambda t: t.astype(jnp.float32)
    lam = (jnp.exp(jnp.sum(f32(lw['lam_q1']) * f32(lw['lam_k1'])))
           - jnp.exp(jnp.sum(f32(lw['lam_q2']) * f32(lw['lam_k2']))) + lam_init)
    o = f32(attend(q, k, v, lam))
    o = o * lax.rsqrt(jnp.mean(o * o, -1, keepdims=True) + SUBLN_EPS)
    o = o * f32(lw['subln_w']).reshape(DF_HEADS, DF_DV) * (1.0 - lam_init)
    mix = jnp.concatenate([rw_out, o.reshape(B, T, DF_W).astype(x.dtype)], -1) @ lw['w_out']
    x = layer_norm(DEEP_ALPHA * x + mix, lw['ln1_w'], lw['ln1_b'])
    ch = peer(x.reshape(B * T, D_MODEL), lw['pk_wq'], lw['pk_subkeys'], lw['peer_u'], lw['peer_v'])
    x = layer_norm(DEEP_ALPHA * x + ch.reshape(B, T, D_MODEL), lw['ln2_w'], lw['ln2_b'])
    return x, k, v, S, rw_cols[:, -1]


def setup_inputs(seed: int = 0) -> dict:
    key = jax.random.key(seed)
    ks = iter(jax.random.split(key, 40))
    nrm = lambda shape, s=1.0: s * jax.random.normal(next(ks), shape, jnp.float32)
    uni = lambda shape, lo, hi: jax.random.uniform(next(ks), shape, jnp.float32, lo, hi)
    n_pages = PAST_LEN // PAGE_SIZE
    n_used = DEC_BATCH * n_pages
    n_pool = n_used + max(1, n_used // 4)
    L = DEPTH
    x_prompt = nrm((BATCH, SEQ, D_MODEL))
    x_sample = nrm((DEC_BATCH, DEC_SEQ, D_MODEL))
    cache_k = nrm((L, n_pool, PAGE_SIZE, DF_HEADS, 2, DF_DK))
    cache_v = nrm((L, n_pool, PAGE_SIZE, DF_HEADS, DF_DV))
    page_table = jax.random.permutation(next(ks), n_pool)[:n_used].reshape(DEC_BATCH, n_pages).astype(jnp.int32)
    state_wkv = nrm((L, DEC_BATCH, RW_HEADS, RW_DH, RW_DH), 0.3)
    state_shift = nrm((L, DEC_BATCH, RW_COLS))
    return {
        'x_prompt': x_prompt,
        'x_sample': x_sample,
        'cache_k': cache_k,
        'cache_v': cache_v,
        'page_table': page_table,
        'state_wkv': state_wkv,
        'state_shift': state_shift,
        'meta_tokens': nrm((N_META, D_MODEL)),
        'w_in': nrm((L, D_MODEL, IN_COLS), D_MODEL ** -0.5),
        'mu_shift': uni((L, RW_COLS), 0.0, 1.0),
        'w0': uni((L, RW_W), -2.0, 1.0),
        'w2_decay': nrm((L, DECAY_LORA, RW_W), 0.1),
        'a0': nrm((L, RW_W), 0.1),
        'a2': nrm((L, AAA_LORA, RW_W), 0.5 * AAA_LORA ** -0.5),
        'g2': nrm((L, GATE_LORA, RW_W), GATE_LORA ** -0.5),
        'k_k': 0.85 + nrm((L, RW_W), 0.05),
        'k_a': 1.0 + nrm((L, RW_W), 0.05),
        'r_k': nrm((L, RW_HEADS, RW_DH), 0.1),
        'lnx_w': 1.0 + nrm((L, RW_W), 0.05),
        'lnx_b': nrm((L, RW_W), 0.01),
        'lam_q1': nrm((L, DF_DK), 0.1),
        'lam_k1': nrm((L, DF_DK), 0.1),
        'lam_q2': nrm((L, DF_DK), 0.1),
        'lam_k2': nrm((L, DF_DK), 0.1),
        'subln_w': 1.0 + nrm((L, DF_W), 0.05),
        'w_out': nrm((L, MIX_W, D_MODEL), DEEP_BETA * MIX_W ** -0.5),
        'ln1_w': 1.0 + nrm((L, D_MODEL), 0.05),
        'ln1_b': nrm((L, D_MODEL), 0.01),
        'pk_wq': nrm((L, D_MODEL, PK_HEADS * PK_DQ), D_MODEL ** -0.5),
        'pk_subkeys': nrm((L, PK_HEADS, 2, N_KEYS, PK_DQ // 2), (PK_DQ // 2) ** -0.5),
        'peer_u': nrm((L, N_EXPERTS, D_MODEL), D_MODEL ** -0.5),
        'peer_v': nrm((L, N_EXPERTS, D_MODEL), DEEP_BETA),
        'ln2_w': 1.0 + nrm((L, D_MODEL), 0.05),
        'ln2_b': nrm((L, D_MODEL), 0.01),
    }


def reference(x_prompt, x_sample, cache_k, cache_v, page_table, state_wkv, state_shift,
              meta_tokens, w_in, mu_shift, w0, w2_decay, a0, a2, g2, k_k, k_a, r_k, lnx_w, lnx_b,
              lam_q1, lam_k1, lam_q2, lam_k2, subln_w, w_out, ln1_w, ln1_b,
              pk_wq, pk_subkeys, peer_u, peer_v, ln2_w, ln2_b):
    B = x_prompt.shape[0]
    meta = jnp.broadcast_to(meta_tokens[None].astype(x_prompt.dtype), (B, N_META, D_MODEL))
    xp = jnp.concatenate([meta, x_prompt], axis=1)
    xs = x_sample
    pos_p = jnp.arange(xp.shape[1])
    past = page_table.shape[1] * cache_k.shape[2]
    pos_s = past + jnp.arange(xs.shape[1])
    wkv0_p = jnp.zeros((B, RW_HEADS, RW_DH, RW_DH), jnp.float32)
    shift0_p = jnp.zeros((B, RW_COLS), xp.dtype)
    kp_l, vp_l, sp_l, hp_l, ks_l, vs_l, ss_l, hs_l = [], [], [], [], [], [], [], []
    for l in range(DEPTH):
        lw = {
            'w_in': w_in[l], 'mu_shift': mu_shift[l], 'w0': w0[l], 'w2_decay': w2_decay[l],
            'a0': a0[l], 'a2': a2[l], 'g2': g2[l], 'k_k': k_k[l], 'k_a': k_a[l], 'r_k': r_k[l],
            'lnx_w': lnx_w[l], 'lnx_b': lnx_b[l], 'lam_q1': lam_q1[l], 'lam_k1': lam_k1[l],
            'lam_q2': lam_q2[l], 'lam_k2': lam_k2[l], 'subln_w': subln_w[l], 'w_out': w_out[l],
            'ln1_w': ln1_w[l], 'ln1_b': ln1_b[l], 'pk_wq': pk_wq[l], 'pk_subkeys': pk_subkeys[l],
            'peer_u': peer_u[l], 'peer_v': peer_v[l], 'ln2_w': ln2_w[l], 'ln2_b': ln2_b[l],
        }
        lam_init = 0.8 - 0.6 * math.exp(-0.3 * l)
        xp, kp, vp, sp, hp = trunk_layer(xp, pos_p, wkv0_p, shift0_p, diff_attn_prompt, lw, lam_init)
        sample_attend = functools.partial(diff_attn_sample, cache_k=cache_k[l], cache_v=cache_v[l],
                                          page_table=page_table)
        xs, ksn, vsn, ssn, hsn = trunk_layer(xs, pos_s, state_wkv[l], state_shift[l], sample_attend, lw, lam_init)
        kp_l.append(kp); vp_l.append(vp); sp_l.append(sp); hp_l.append(hp)
        ks_l.append(ksn); vs_l.append(vsn); ss_l.append(ssn); hs_l.append(hsn)
    y_prompt = xp[:, N_META:]
    y_sample = xs
    return (y_prompt, y_sample,
            jnp.stack(kp_l), jnp.stack(vp_l), jnp.stack(sp_l), jnp.stack(hp_l),
            jnp.stack(ks_l), jnp.stack(vs_l), jnp.stack(ss_l), jnp.stack(hs_l))
```

```python
import functools
import math

import jax
import jax.numpy as jnp
from jax import lax
from jax.experimental import pallas as pl
from jax.experimental.pallas import tpu as pltpu

F32 = jnp.float32
BF16 = jnp.bfloat16

N_META = 16
RW_HEADS = 8
RW_DH = 64
RW_W = RW_HEADS * RW_DH
DECAY_LORA = 64
AAA_LORA = 64
GATE_LORA = 128
RW_COLS = 3 * RW_W + DECAY_LORA + AAA_LORA + GATE_LORA
LNX_EPS = 64e-5
DF_HEADS = 4
DF_DK = 64
DF_DV = 2 * DF_DK
DF_W = DF_HEADS * DF_DV
ROPE_THETA = 10000.0
SUBLN_EPS = 1e-5
LN_EPS = 1e-5
PK_HEADS = 8
N_KEYS = 128
PK_DQ = 128
PK_TOPK = 16
DEPTH = 1
DEEP_ALPHA = (2 * DEPTH) ** 0.25

LANES = 128
SUBLANES = 8
VMEM_LIMIT = 56 * 1024 * 1024

ROW_TILE = 512
PREP_TILE = 128
WKV_CHUNK = 64
ATT_TILE = 384
PEER_TOK = 512
PEER_EXP = 1024
HI = lax.Precision.HIGHEST


def _cparams(sem):
    return pltpu.CompilerParams(dimension_semantics=sem, vmem_limit_bytes=VMEM_LIMIT)


def _dot(a, b, dims=((1,), (0,)), precision=None):
    return lax.dot_general(a, b, (dims, ((), ())), precision=precision,
                           preferred_element_type=F32)


def _dot3(a, b, dims=((1,), (0,))):
    a_hi = a.astype(BF16)
    a_lo = (a - a_hi.astype(F32)).astype(BF16)
    b_hi = b.astype(BF16)
    b_lo = (b - b_hi.astype(F32)).astype(BF16)
    return _dot(a_hi, b_hi, dims) + (_dot(a_lo, b_hi, dims) + _dot(a_hi, b_lo, dims))


def _rope128(z, cos, sin_signed, first_half):
    partner = jnp.where(first_half, pltpu.roll(z, LANES - 32, 1), pltpu.roll(z, 32, 1))
    return z * cos + partner * sin_signed


def _inproj_kernel(x_ref, w_ref, cos_ref, sin_ref, rw_ref, q_ref, k_ref, v_ref):
    x = x_ref[...].astype(BF16)
    rw_ref[...] = _dot(x, w_ref[:, :RW_COLS])
    cos = cos_ref[...]
    sin = sin_ref[...]
    lane = lax.broadcasted_iota(jnp.int32, cos.shape, 1)
    first_half = (lane & 32) == 0
    q = _dot(x, w_ref[:, RW_COLS:RW_COLS + DF_W])
    k = _dot(x, w_ref[:, RW_COLS + DF_W:RW_COLS + 2 * DF_W])
    for g in range(DF_W // LANES):
        sl = slice(g * LANES, (g + 1) * LANES)
        q_ref[:, sl] = _rope128(q[:, sl], cos, sin, first_half)
        k_ref[:, sl] = _rope128(k[:, sl], cos, sin, first_half)
    v_ref[...] = _dot(x, w_ref[:, RW_COLS + 2 * DF_W:])


def _inproj(x, w_bf, cos, sin):
    rows, d = x.shape
    in_cols = w_bf.shape[1]
    tm = ROW_TILE
    row = lambda i: (i, 0)
    return pl.pallas_call(
        _inproj_kernel,
        grid=(rows // tm,),
        in_specs=[pl.BlockSpec((tm, d), row),
                  pl.BlockSpec((d, in_cols), lambda i: (0, 0)),
                  pl.BlockSpec((tm, LANES), row),
                  pl.BlockSpec((tm, LANES), row)],
        out_specs=[pl.BlockSpec((tm, RW_COLS), row),
                   pl.BlockSpec((tm, DF_W), row),
                   pl.BlockSpec((tm, DF_W), row),
                   pl.BlockSpec((tm, DF_W), row)],
        out_shape=[jax.ShapeDtypeStruct((rows, RW_COLS), F32),
                   jax.ShapeDtypeStruct((rows, DF_W), F32),
                   jax.ShapeDtypeStruct((rows, DF_W), F32),
                   jax.ShapeDtypeStruct((rows, DF_W), F32)],
        compiler_params=_cparams(("parallel",)),
        name="inproj_rope",
    )(x, w_bf, cos, sin)


def _prep_kernel(t_real, cols_ref, shift0_ref, mu_ref, w0_ref, w2_ref, a0_ref, a2_ref, g2_ref,
                 kk_ref, ka_ref, headsum_ref,
                 r_ref, lw_ref, k_ref, v_ref, kkn_ref, b_ref, g_ref, carry_ref):
    ti = pl.program_id(1)
    tm = cols_ref.shape[1]

    @pl.when(ti == 0)
    def _():
        carry_ref[...] = jnp.broadcast_to(shift0_ref[0], carry_ref.shape)

    x = cols_ref[0]
    rowi = lax.broadcasted_iota(jnp.int32, x.shape, 0)
    prev = jnp.where(rowi == 0, carry_ref[0:1, :], pltpu.roll(x, 1, 0))
    carry_ref[...] = jnp.broadcast_to(x[tm - 1:tm, :], carry_ref.shape)
    mixed = x + (prev - x) * mu_ref[...]
    r = mixed[:, 0:RW_W]
    k = mixed[:, RW_W:2 * RW_W]
    v = mixed[:, 2 * RW_W:3 * RW_W]
    o = 3 * RW_W
    xw = mixed[:, o:o + DECAY_LORA]
    xa = mixed[:, o + DECAY_LORA:o + DECAY_LORA + AAA_LORA]
    xg = mixed[:, o + DECAY_LORA + AAA_LORA:]
    d = w0_ref[...] + _dot3(jnp.tanh(xw), w2_ref[...])
    lw = -jax.nn.sigmoid(d) * math.exp(-0.5)
    a = jax.nn.sigmoid(a0_ref[...] + _dot3(xa, a2_ref[...]))
    g = _dot3(jax.nn.sigmoid(xg), g2_ref[...])
    kk = k * kk_ref[...]
    k2 = k * (1.0 + (a - 1.0) * ka_ref[...])
    ssq = _dot3(kk * kk, headsum_ref[...])
    kkn = kk / jnp.maximum(jnp.sqrt(ssq), 1e-12)
    valid = (ti * tm + lax.broadcasted_iota(jnp.int32, (tm, RW_W), 0)) < t_real
    r_ref[0] = r
    lw_ref[0] = jnp.where(valid, lw, 0.0)
    k_ref[0] = jnp.where(valid, k2, 0.0)
    v_ref[0] = v
    kkn_ref[0] = jnp.where(valid, kkn, 0.0)
    b_ref[0] = jnp.where(valid, kkn * a, 0.0)
    g_ref[0] = g


def _rwkv_prep(cols, shift0, t_real, p):
    bx, tx, _ = cols.shape
    tm = min(PREP_TILE, tx)
    blk = lambda b, t: (b, t, 0)
    vec = lambda n: pl.BlockSpec((1, n), lambda b, t: (0, 0))
    mat = lambda r, c: pl.BlockSpec((r, c), lambda b, t: (0, 0))
    outs = [jax.ShapeDtypeStruct((bx, tx, RW_W), F32)] * 7
    return pl.pallas_call(
        functools.partial(_prep_kernel, t_real),
        grid=(bx, tx // tm),
        in_specs=[pl.BlockSpec((1, tm, RW_COLS), blk),
                  pl.BlockSpec((1, 1, RW_COLS), lambda b, t: (b, 0, 0)),
                  vec(RW_COLS), vec(RW_W), mat(DECAY_LORA, RW_W), vec(RW_W), mat(AAA_LORA, RW_W),
                  mat(GATE_LORA, RW_W), vec(RW_W), vec(RW_W), mat(RW_W, RW_W)],
        out_specs=[pl.BlockSpec((1, tm, RW_W), blk)] * 7,
        out_shape=outs,
        scratch_shapes=[pltpu.VMEM((SUBLANES, RW_COLS), F32)],
        compiler_params=_cparams(("parallel", "arbitrary")),
        name="rwkv_prep",
    )(cols, shift0.reshape(bx, 1, RW_COLS), p["mu_shift"], p["w0"], p["w2_decay"], p["a0"], p["a2"],
      p["g2"], p["k_k"], p["k_a"], p["headsum"])


def _wkv_kernel(r_ref, lw_ref, k_ref, v_ref, kkn_ref, b_ref, g_ref, s0_ref, rk_ref, lnw_ref, lnb_ref,
                out_ref, s_ref):
    ci = pl.program_id(1)
    c = r_ref.shape[1]

    @pl.when(ci == 0)
    def _():
        s_ref[...] = s0_ref[...]

    row = lax.broadcasted_iota(jnp.int32, (c, c), 0)
    col = lax.broadcasted_iota(jnp.int32, (c, c), 1)
    lower_incl = col <= row
    lower_strict = col < row
    lw = lw_ref[0]
    cum = _dot3(lower_incl.astype(F32), lw)
    cum_last = cum[c - 1:c, :]
    r = r_ref[0]
    k = k_ref[0]
    v = v_ref[0]
    kkn = kkn_ref[0]
    b = b_ref[0]
    g_out = jnp.exp(-cum)
    g_end = jnp.exp(cum_last - cum)
    a_hat = -kkn * jnp.exp(cum - lw)
    b_hat = b * g_out
    k_hat = k * g_out
    r_hat = r * jnp.exp(cum)
    b_end = b * g_end
    k_end = k * g_end
    gamma = jnp.exp(cum_last)
    rkk = r * k * rk_ref[...]

    row2 = lax.broadcasted_iota(jnp.int32, (2 * c, 2 * c), 0)
    col2 = lax.broadcasted_iota(jnp.int32, (2 * c, 2 * c), 1)
    rt = jnp.where(row2 >= c, row2 - c, row2)
    cs = jnp.where(col2 >= c, col2 - c, col2)
    keep = cs < rt + jnp.where(row2 >= c, 1, 0)

    for h in range(RW_HEADS):
        sl = slice(h * RW_DH, (h + 1) * RW_DH)
        s0 = s_ref[0, h]
        x_st = jnp.concatenate([a_hat[:, sl], r_hat[:, sl]], axis=0)
        y_st = jnp.concatenate([b_hat[:, sl], k_hat[:, sl]], axis=0)
        prod = jnp.where(keep, _dot3(x_st, y_st, ((1,), (1,))), 0.0)
        xs = _dot3(x_st, s0, ((1,), (1,)))
        v_h = v[:, sl]
        l_ab = prod[:c, :c]
        z = xs[:c] + _dot3(prod[:c, c:], v_h)
        lp = l_ab
        n_sq = int(math.log2(c))
        for i in range(n_sq):
            z = z + _dot3(lp, z)
            if i + 1 < n_sq:
                lp = _dot3(lp, lp)
        uv = jnp.concatenate([z, v_h], axis=0)
        y = xs[c:] + _dot3(prod[c:, :], uv)
        e_st = jnp.concatenate([b_end[:, sl], k_end[:, sl]], axis=0)
        s_ref[0, h] = s0 * gamma[:, sl] + _dot3(uv, e_st, ((0,), (0,)))
        mu = jnp.mean(y, axis=-1, keepdims=True)
        yc = y - mu
        var = jnp.mean(yc * yc, axis=-1, keepdims=True)
        yn = yc * lax.rsqrt(var + LNX_EPS) * lnw_ref[:, sl] + lnb_ref[:, sl]
        bonus = jnp.sum(rkk[:, sl], axis=-1, keepdims=True) * v_h
        out_ref[0, :, sl] = (yn + bonus) * g_ref[0, :, sl]


def _wkv(prep, s0, p):
    r = prep[0]
    bx, tx, _ = r.shape
    c = WKV_CHUNK
    blk = pl.BlockSpec((1, c, RW_W), lambda b, t: (b, t, 0))
    vec = pl.BlockSpec((1, RW_W), lambda b, t: (0, 0))
    st = pl.BlockSpec((1, RW_HEADS, RW_DH, RW_DH), lambda b, t: (b, 0, 0, 0))
    return pl.pallas_call(
        _wkv_kernel,
        grid=(bx, tx // c),
        in_specs=[blk] * 7 + [st, vec, vec, vec],
        out_specs=[blk, st],
        out_shape=[jax.ShapeDtypeStruct((bx, tx, RW_W), F32),
                   jax.ShapeDtypeStruct((bx, RW_HEADS, RW_DH, RW_DH), F32)],
        compiler_params=_cparams(("parallel", "arbitrary")),
        name="wkv_chunk",
    )(*prep, s0, p["r_k"], p["lnx_w"], p["lnx_b"])


def _attn_kernel(lam_init, lam_ref, q_ref, k_ref, v_ref, sw_ref, o_ref, kb_ref, vb_ref,
                 m_ref, l_ref, acc_ref):
    qi = pl.program_id(2)
    tq = q_ref.shape[1]

    @pl.when(qi == 0)
    def _():
        kb_ref[...] = k_ref[0].astype(BF16)
        vb_ref[...] = v_ref[0].astype(BF16)

    q = (q_ref[0] * (DF_DK ** -0.5)).astype(BF16)
    m_ref[...] = jnp.full(m_ref.shape, -jnp.inf, F32)
    l_ref[...] = jnp.zeros(l_ref.shape, F32)
    acc_ref[...] = jnp.zeros(acc_ref.shape, F32)
    rowi = lax.broadcasted_iota(jnp.int32, (tq, tq), 0)
    coli = lax.broadcasted_iota(jnp.int32, (tq, tq), 1)
    causal = coli <= rowi

    def block(kj, masked):
        start = pl.multiple_of(kj * tq, tq)
        kblk = kb_ref[pl.ds(start, tq), :]
        vblk = vb_ref[pl.ds(start, tq), :]
        for c in range(2):
            cs = slice(c * DF_DK, (c + 1) * DF_DK)
            s = _dot(q[:, cs], kblk[:, cs], ((1,), (1,)))
            if masked:
                s = jnp.where(causal, s, -jnp.inf)
            m_old = m_ref[c]
            m_new = jnp.maximum(m_old, jnp.max(s, axis=-1, keepdims=True))
            alpha = jnp.exp(m_old - m_new)
            pr = jnp.exp(s - m_new)
            l_ref[c] = alpha * l_ref[c] + jnp.sum(pr, axis=-1, keepdims=True)
            acc_ref[c] = alpha * acc_ref[c] + _dot(pr.astype(BF16), vblk)
            m_ref[c] = m_new

    def body(kj, carry):
        block(kj, False)
        return carry

    lax.fori_loop(0, qi, body, 0)
    block(qi, True)
    o = acc_ref[0] / l_ref[0] - lam_ref[0] * (acc_ref[1] / l_ref[1])
    o = o * lax.rsqrt(jnp.mean(o * o, axis=-1, keepdims=True) + SUBLN_EPS)
    o_ref[0] = o * sw_ref[...] * (1.0 - lam_init)


def _attn_prompt(q, k, v, lam, subln_w, lam_init):
    b, tp, _ = q.shape
    tq = ATT_TILE
    return pl.pallas_call(
        functools.partial(_attn_kernel, lam_init),
        grid=(b, DF_HEADS, tp // tq),
        in_specs=[pl.BlockSpec(memory_space=pltpu.SMEM),
                  pl.BlockSpec((1, tq, DF_DV), lambda bi, h, i: (bi, i, h)),
                  pl.BlockSpec((1, tp, DF_DV), lambda bi, h, i: (bi, 0, h)),
                  pl.BlockSpec((1, tp, DF_DV), lambda bi, h, i: (bi, 0, h)),
                  pl.BlockSpec((1, DF_DV), lambda bi, h, i: (0, h))],
        out_specs=pl.BlockSpec((1, tq, DF_DV), lambda bi, h, i: (bi, i, h)),
        out_shape=jax.ShapeDtypeStruct((b, tp, DF_W), F32),
        scratch_shapes=[pltpu.VMEM((tp, DF_DV), BF16), pltpu.VMEM((tp, DF_DV), BF16),
                        pltpu.VMEM((2, tq, 1), F32), pltpu.VMEM((2, tq, 1), F32),
                        pltpu.VMEM((2, tq, DF_DV), F32)],
        compiler_params=_cparams(("parallel", "parallel", "arbitrary")),
        name="diff_attn_prompt",
    )(lam, q, k, v, subln_w)


def _decode_kernel(lam_init, pt_ref, lam_ref, q_ref, kn_ref, vn_ref, ck_ref, cv_ref, sw_ref, o_ref,
                   qrow_ref, m_ref, l_ref, acc_ref):
    pi = pl.program_id(1)
    n_pages = pl.num_programs(1)
    nq = 2 * DF_HEADS

    @pl.when(pi == 0)
    def _():
        rowi = lax.broadcasted_iota(jnp.int32, (nq, DF_W), 0)
        coli = lax.broadcasted_iota(jnp.int32, (nq, DF_W), 1)
        qb = jnp.broadcast_to(q_ref[0] * (DF_DK ** -0.5), (nq, DF_W))
        qrow_ref[...] = jnp.where(jnp.right_shift(coli, int(math.log2(DF_DK))) == rowi, qb, 0.0)
        m_ref[...] = jnp.full(m_ref.shape, -jnp.inf, F32)
        l_ref[...] = jnp.zeros(l_ref.shape, F32)
        acc_ref[...] = jnp.zeros(acc_ref.shape, F32)

    qrow = qrow_ref[...]
    s = _dot(qrow.astype(BF16), ck_ref[0].astype(BF16), ((1,), (1,)))
    m_old = m_ref[...]
    m_new = jnp.maximum(m_old, jnp.max(s, axis=-1, keepdims=True))
    alpha = jnp.exp(m_old - m_new)
    pr = jnp.exp(s - m_new)
    l_ref[...] = alpha * l_ref[...] + jnp.sum(pr, axis=-1, keepdims=True)
    acc_ref[...] = alpha * acc_ref[...] + _dot(pr.astype(BF16), cv_ref[0].astype(BF16))
    m_ref[...] = m_new

    @pl.when(pi == n_pages - 1)
    def _():
        s_self = jnp.sum(qrow * kn_ref[0], axis=-1, keepdims=True)
        m_old = m_ref[...]
        m_fin = jnp.maximum(m_old, s_self)
        alpha = jnp.exp(m_old - m_fin)
        p_self = jnp.exp(s_self - m_fin)
        l_fin = alpha * l_ref[...] + p_self
        acc = (alpha * acc_ref[...] + p_self * vn_ref[0]) / l_fin
        for h in range(DF_HEADS):
            sl = slice(h * DF_DV, (h + 1) * DF_DV)
            o = acc[2 * h:2 * h + 1, sl] - lam_ref[0] * acc[2 * h + 1:2 * h + 2, sl]
            o = o * lax.rsqrt(jnp.mean(o * o, axis=-1, keepdims=True) + SUBLN_EPS)
            o_ref[0, :, sl] = o * sw_ref[:, sl] * (1.0 - lam_init)


def _attn_sample(q, k_new, v_new, cache_k, cache_v, page_table, lam, subln_w, lam_init):
    bs = q.shape[0]
    n_pages = page_table.shape[1]
    page = cache_k.shape[1]
    row = pl.BlockSpec((1, 1, DF_W), lambda b, p, pt: (b, 0, 0))
    pg = pl.BlockSpec((1, page, DF_W), lambda b, p, pt: (pt[b, p], 0, 0))
    nq = 2 * DF_HEADS
    grid_spec = pltpu.PrefetchScalarGridSpec(
        num_scalar_prefetch=1,
        grid=(bs, n_pages),
        in_specs=[pl.BlockSpec(memory_space=pltpu.SMEM), row, row, row, pg, pg,
                  pl.BlockSpec((1, DF_W), lambda b, p, pt: (0, 0))],
        out_specs=row,
        scratch_shapes=[pltpu.VMEM((nq, DF_W), F32), pltpu.VMEM((nq, 1), F32),
                        pltpu.VMEM((nq, 1), F32), pltpu.VMEM((nq, DF_W), F32)],
    )
    out = pl.pallas_call(
        functools.partial(_decode_kernel, lam_init),
        grid_spec=grid_spec,
        out_shape=jax.ShapeDtypeStruct((bs, 1, DF_W), F32),
        compiler_params=_cparams(("parallel", "arbitrary")),
        name="diff_attn_paged",
    )(page_table, lam, q.reshape(bs, 1, DF_W), k_new.reshape(bs, 1, DF_W), v_new.reshape(bs, 1, DF_W),
      cache_k, cache_v, subln_w)
    return out.reshape(bs, DF_W)


def _layer_norm(x, w, b):
    mu = jnp.mean(x, axis=-1, keepdims=True)
    xc = x - mu
    var = jnp.mean(xc * xc, axis=-1, keepdims=True)
    return xc * lax.rsqrt(var + LN_EPS) * w + b


def _outproj_kernel(x_ref, rw_ref, o_ref, w_ref, lnw_ref, lnb_ref, y_ref):
    mix = _dot(rw_ref[...].astype(BF16), w_ref[:RW_W, :]) + _dot(o_ref[...].astype(BF16), w_ref[RW_W:, :])
    y_ref[...] = _layer_norm(DEEP_ALPHA * x_ref[...] + mix, lnw_ref[...], lnb_ref[...])


def _outproj(x, rw_out, o, w_bf, ln_w, ln_b):
    rows, d = x.shape
    tm = ROW_TILE
    row = lambda i: (i, 0)
    return pl.pallas_call(
        _outproj_kernel,
        grid=(rows // tm,),
        in_specs=[pl.BlockSpec((tm, d), row), pl.BlockSpec((tm, RW_W), row), pl.BlockSpec((tm, DF_W), row),
                  pl.BlockSpec(w_bf.shape, lambda i: (0, 0)),
                  pl.BlockSpec((1, d), lambda i: (0, 0)), pl.BlockSpec((1, d), lambda i: (0, 0))],
        out_specs=pl.BlockSpec((tm, d), row),
        out_shape=jax.ShapeDtypeStruct((rows, d), F32),
        compiler_params=_cparams(("parallel",)),
        name="outproj_ln",
    )(x, rw_out, o, w_bf, ln_w, ln_b)


def _peer_pairs():
    n = PK_TOPK + 1
    return [(i, j) for i in range(n) for j in range(n) if (i + 1) * (j + 1) <= n]


def _extract_top(cur, n, n_pad):
    vals = []
    rowi = lax.broadcasted_iota(jnp.int32, (n_pad, cur.shape[1]), 0)
    stacked = jnp.zeros((n_pad, cur.shape[1]), F32)
    for i in range(n):
        m = jnp.max(cur, axis=0, keepdims=True)
        vals.append(m)
        stacked = jnp.where(rowi == i, m, stacked)
        cur = jnp.where(cur == m, -jnp.inf, cur)
    return vals, stacked


def _peer_score_kernel(x_ref, wq_ref, sk_ref, sela_ref, selb_ref,
                       e1_ref, e2_ref, phi_ref, s2_ref, qt_ref):
    n_top = PK_TOPK + 1
    half = PK_DQ // 2
    tt = x_ref.shape[0]
    qt_ref[...] = _dot3(wq_ref[...], x_ref[...], ((1,), (1,)))
    n_cand = sela_ref.shape[0]
    n_pad = sela_ref.shape[1]
    cand_row = lax.broadcasted_iota(jnp.int32, (n_cand, tt), 0)
    n_pairs = len(_peer_pairs())
    for h in range(PK_HEADS):
        s1 = _dot3(sk_ref[h, 0], qt_ref[pl.ds(h * PK_DQ, half), :])
        s2 = _dot3(sk_ref[h, 1], qt_ref[pl.ds(h * PK_DQ + half, half), :])
        top1, a_st = _extract_top(s1, n_top, n_pad)
        top2, b_st = _extract_top(s2, n_top, n_pad)
        cand = _dot(sela_ref[...], a_st, precision=HI) + _dot(selb_ref[...], b_st, precision=HI)
        cand = jnp.where(cand_row < n_pairs, cand, -jnp.inf)
        ctop, _ = _extract_top(cand, n_top, n_pad)
        cmax = ctop[0]
        theta = 0.5 * (ctop[PK_TOPK - 1] + ctop[PK_TOPK])
        z = jnp.sum(jnp.where(cand >= theta, jnp.exp(cand - cmax), 0.0), axis=0, keepdims=True)
        e1_ref[h] = jnp.exp(s1 - top1[0]) / z
        e2_ref[h] = jnp.exp(s2 - top2[0])
        phi_ref[h] = theta - s1
        s2_ref[h] = s2


def _peer_scores(x1, wq_t, subkeys, sel_a, sel_b):
    rows, d = x1.shape
    tt = PEER_TOK
    out = jax.ShapeDtypeStruct((PK_HEADS, N_KEYS, rows), F32)
    ospec = pl.BlockSpec((PK_HEADS, N_KEYS, tt), lambda i: (0, 0, i))
    return pl.pallas_call(
        _peer_score_kernel,
        grid=(rows // tt,),
        in_specs=[pl.BlockSpec((tt, d), lambda i: (i, 0)),
                  pl.BlockSpec(wq_t.shape, lambda i: (0, 0)),
                  pl.BlockSpec(subkeys.shape, lambda i: (0, 0, 0, 0)),
                  pl.BlockSpec(sel_a.shape, lambda i: (0, 0)),
                  pl.BlockSpec(sel_b.shape, lambda i: (0, 0))],
        out_specs=[ospec] * 4,
        out_shape=[out] * 4,
        scratch_shapes=[pltpu.VMEM((PK_HEADS * PK_DQ, tt), F32)],
        compiler_params=_cparams(("parallel",)),
        name="peer_scores",
    )(x1, wq_t, subkeys, sel_a, sel_b)


def _peer_expert_kernel(x_ref, xb_ref, u_ref, vt_ref, e1_ref, e2_ref, phi_ref, s2_ref, lnw_ref, lnb_ref,
                        y_ref, wt_ref, acc_ref):
    ei = pl.program_id(1)
    te = u_ref.shape[0]
    tt = x_ref.shape[0]

    @pl.when(ei == 0)
    def _():
        acc_ref[...] = jnp.zeros(acc_ref.shape, F32)

    xb = xb_ref[...]

    def row_body(j, carry):
        i1 = ei * (te // N_KEYS) + j
        off = pl.multiple_of(j * N_KEYS, N_KEYS)
        act = _dot(u_ref[pl.ds(off, N_KEYS), :], xb, ((1,), (1,)))
        gate = jnp.zeros((N_KEYS, tt), F32)
        for h in range(PK_HEADS):
            sel = s2_ref[h] >= phi_ref[h, pl.ds(i1, 1), :]
            gate = gate + jnp.where(sel, e1_ref[h, pl.ds(i1, 1), :] * e2_ref[h], 0.0)
        gelu = 0.5 * act * (1.0 + lax.erf(act * (2.0 ** -0.5)))
        wt_ref[pl.ds(off, N_KEYS), :] = (gate * gelu).astype(BF16)
        return carry

    lax.fori_loop(0, te // N_KEYS, row_body, 0)
    acc_ref[...] += _dot(vt_ref[...], wt_ref[...])

    @pl.when(ei == pl.num_programs(1) - 1)
    def _():
        ch = acc_ref[...].T
        y_ref[...] = _layer_norm(DEEP_ALPHA * x_ref[...] + ch, lnw_ref[...], lnb_ref[...])


def _peer_experts(x1, x1_bf, u_bf, vt_bf, e1, e2, phi, s2, ln_w, ln_b):
    rows, d = x1.shape
    n_exp = u_bf.shape[0]
    tt = PEER_TOK
    te = PEER_EXP
    tok = lambda i, e: (i, 0)
    sc = pl.BlockSpec((PK_HEADS, N_KEYS, tt), lambda i, e: (0, 0, i))
    vec = pl.BlockSpec((1, d), lambda i, e: (0, 0))
    return pl.pallas_call(
        _peer_expert_kernel,
        grid=(rows // tt, n_exp // te),
        in_specs=[pl.BlockSpec((tt, d), tok), pl.BlockSpec((tt, d), tok),
                  pl.BlockSpec((te, d), lambda i, e: (e, 0)),
                  pl.BlockSpec((d, te), lambda i, e: (0, e)),
                  sc, sc, sc, sc, vec, vec],
        out_specs=pl.BlockSpec((tt, d), tok),
        out_shape=jax.ShapeDtypeStruct((rows, d), F32),
        scratch_shapes=[pltpu.VMEM((te, tt), BF16), pltpu.VMEM((d, tt), F32)],
        compiler_params=_cparams(("parallel", "arbitrary")),
        name="peer_experts",
    )(x1, x1_bf, u_bf, vt_bf, e1, e2, phi, s2, ln_w, ln_b)


def _rope_tables(pos):
    half = DF_DK // 2
    inv = ROPE_THETA ** (-jnp.arange(half, dtype=F32) / half)
    ang = pos.astype(F32)[:, None] * inv[None, :]
    cos, sin = jnp.cos(ang), jnp.sin(ang)
    cos128 = jnp.concatenate([cos, cos, cos, cos], axis=1)
    sin128 = jnp.concatenate([-sin, sin, -sin, sin], axis=1)
    return cos128, sin128


def _round_up(n, m):
    return -(-n // m) * m


def kernel(x_prompt, x_sample, cache_k, cache_v, page_table, state_wkv, state_shift, meta_tokens, w_in,
           mu_shift, w0, w2_decay, a0, a2, g2, k_k, k_a, r_k, lnx_w, lnx_b, lam_q1, lam_k1, lam_q2, lam_k2,
           subln_w, w_out, ln1_w, ln1_b, pk_wq, pk_subkeys, peer_u, peer_v, ln2_w, ln2_b):
    assert w_in.shape[0] == DEPTH == 1
    bp, seq, d = x_prompt.shape
    bs, dec_seq, _ = x_sample.shape
    assert dec_seq == 1
    t_prompt = N_META + seq
    tp = _round_up(t_prompt, math.lcm(ATT_TILE, PREP_TILE, WKV_CHUNK))
    n_prompt_rows = bp * tp
    rows = _round_up(n_prompt_rows + bs, ROW_TILE)
    page = cache_k.shape[2]
    past = page_table.shape[1] * page
    lam_init = 0.8 - 0.6 * math.exp(-0.3 * 0)
    row2 = lambda a: a.reshape(1, -1)

    meta = jnp.broadcast_to(meta_tokens[None].astype(x_prompt.dtype), (bp, N_META, d))
    xp = jnp.concatenate([meta, x_prompt, jnp.zeros((bp, tp - t_prompt, d), x_prompt.dtype)], axis=1)
    x_all = jnp.concatenate([xp.reshape(n_prompt_rows, d), x_sample.reshape(bs, d),
                             jnp.zeros((rows - n_prompt_rows - bs, d), x_prompt.dtype)], axis=0)
    pos = jnp.concatenate([jnp.tile(jnp.arange(tp), bp), jnp.full((rows - n_prompt_rows,), past)])
    cos, sin = _rope_tables(pos)

    rw_cols, q, k, v = _inproj(x_all, w_in[0].astype(BF16), cos, sin)

    pr = lambda a: a[:n_prompt_rows].reshape(bp, tp, -1)
    sm = lambda a: a[n_prompt_rows:n_prompt_rows + bs]

    hs = jnp.arange(RW_W) // RW_DH
    rw_params = {
        "mu_shift": row2(mu_shift[0]), "w0": row2(w0[0]), "w2_decay": w2_decay[0], "a0": row2(a0[0]),
        "a2": a2[0], "g2": g2[0], "k_k": row2(k_k[0]), "k_a": row2(k_a[0]),
        "headsum": (hs[:, None] == hs[None, :]).astype(F32),
        "r_k": row2(r_k[0]), "lnx_w": row2(lnx_w[0]), "lnx_b": row2(lnx_b[0]),
    }
    rw_p = pr(rw_cols)
    prep_p = _rwkv_prep(rw_p, jnp.zeros((bp, RW_COLS), F32), t_prompt, rw_params)
    rw_out_p, wkv_p = _wkv(prep_p, jnp.zeros((bp, RW_HEADS, RW_DH, RW_DH), F32), rw_params)
    ts = WKV_CHUNK
    rw_s = jnp.concatenate([sm(rw_cols)[:, None, :], jnp.zeros((bs, ts - 1, RW_COLS), F32)], axis=1)
    prep_s = _rwkv_prep(rw_s, state_shift[0], dec_seq, rw_params)
    rw_out_s, wkv_s = _wkv(prep_s, state_wkv[0], rw_params)

    f32 = lambda t: t.astype(F32)
    lam = (jnp.exp(jnp.sum(f32(lam_q1[0]) * f32(lam_k1[0])))
           - jnp.exp(jnp.sum(f32(lam_q2[0]) * f32(lam_k2[0]))) + lam_init).reshape(1)
    sw = row2(subln_w[0])
    o_p = _attn_prompt(pr(q), pr(k), pr(v), lam, sw, lam_init)
    n_pool = cache_k.shape[1]
    o_s = _attn_sample(sm(q), sm(k), sm(v), cache_k[0].reshape(n_pool, page, DF_W),
                       cache_v[0].reshape(n_pool, page, DF_W), page_table, lam, sw, lam_init)

    tail = jnp.zeros((rows - n_prompt_rows - bs, RW_W), F32)
    rw_out = jnp.concatenate([rw_out_p.reshape(n_prompt_rows, RW_W), rw_out_s[:, 0, :], tail], axis=0)
    o_all = jnp.concatenate([o_p.reshape(n_prompt_rows, DF_W), o_s, tail], axis=0)
    x1 = _outproj(x_all, rw_out, o_all, w_out[0].astype(BF16), row2(ln1_w[0]), row2(ln1_b[0]))

    pairs = _peer_pairs()
    n_cand = _round_up(len(pairs), SUBLANES)
    n_pad = _round_up(PK_TOPK + 1, SUBLANES)
    ia = jnp.array([p[0] for p in pairs] + [0] * (n_cand - len(pairs)))
    ib = jnp.array([p[1] for p in pairs] + [0] * (n_cand - len(pairs)))
    sel_a = (ia[:, None] == jnp.arange(n_pad)[None, :]).astype(F32)
    sel_b = (ib[:, None] == jnp.arange(n_pad)[None, :]).astype(F32)
    e1, e2, phi, s2 = _peer_scores(x1, pk_wq[0].T, pk_subkeys[0], sel_a, sel_b)
    y = _peer_experts(x1, x1.astype(BF16), peer_u[0].astype(BF16), peer_v[0].T.astype(BF16),
                      e1, e2, phi, s2, row2(ln2_w[0]), row2(ln2_b[0]))

    y_prompt = pr(y)[:, N_META:t_prompt]
    y_sample = sm(y).reshape(bs, dec_seq, d)
    k_p = pr(k)[:, :t_prompt].reshape(1, bp, t_prompt, DF_HEADS, 2, DF_DK)
    v_p = pr(v)[:, :t_prompt].reshape(1, bp, t_prompt, DF_HEADS, DF_DV)
    shift_p = rw_p[:, t_prompt - 1][None]
    k_s = sm(k).reshape(1, bs, dec_seq, DF_HEADS, 2, DF_DK)
    v_s = sm(v).reshape(1, bs, dec_seq, DF_HEADS, DF_DV)
    shift_s = sm(rw_cols)[None]
    return (y_prompt, y_sample, k_p, v_p, wkv_p[None], shift_p, k_s, v_s, wkv_s[None], shift_s)
```

```python
import functools
import math

import jax
import jax.numpy as jnp
from jax import lax
from jax.experimental import pallas as pl
from jax.experimental.pallas import tpu as pltpu

F32 = jnp.float32
BF16 = jnp.bfloat16

N_META = 16
RW_HEADS = 8
RW_DH = 64
RW_W = RW_HEADS * RW_DH
DECAY_LORA = 64
AAA_LORA = 64
GATE_LORA = 128
RW_COLS = 3 * RW_W + DECAY_LORA + AAA_LORA + GATE_LORA
LNX_EPS = 64e-5
DF_HEADS = 4
DF_DK = 64
DF_DV = 2 * DF_DK
DF_W = DF_HEADS * DF_DV
ROPE_THETA = 10000.0
SUBLN_EPS = 1e-5
LN_EPS = 1e-5
PK_HEADS = 8
N_KEYS = 128
PK_DQ = 128
PK_TOPK = 16
DEPTH = 1
DEEP_ALPHA = (2 * DEPTH) ** 0.25

LANES = 128
SUBLANES = 8
VMEM_LIMIT = 56 * 1024 * 1024

ROW_TILE = 512
PREP_TILE = 128
WKV_CHUNK = 64
WKV_BATCH = 4
ATT_TILE = 384
DEC_PAGES = 8
PEER_TOK = 512
PEER_EXP = 1024
PEER_KSPLIT = 256
HI = lax.Precision.HIGHEST


def _cparams(sem):
    return pltpu.CompilerParams(dimension_semantics=sem, vmem_limit_bytes=VMEM_LIMIT)


def _dot(a, b, dims=((1,), (0,)), precision=None):
    return lax.dot_general(a, b, (dims, ((), ())), precision=precision,
                           preferred_element_type=F32)


def _dot3(a, b, dims=((1,), (0,))):
    a_hi = a.astype(BF16)
    a_lo = (a - a_hi.astype(F32)).astype(BF16)
    b_hi = b.astype(BF16)
    b_lo = (b - b_hi.astype(F32)).astype(BF16)
    return _dot(a_hi, b_hi, dims) + (_dot(a_lo, b_hi, dims) + _dot(a_hi, b_lo, dims))


def _rope128(z, cos, sin_signed, first_half):
    partner = jnp.where(first_half, pltpu.roll(z, LANES - 32, 1), pltpu.roll(z, 32, 1))
    return z * cos + partner * sin_signed


def _inproj_kernel(x_ref, w_ref, cos_ref, sin_ref, rw_ref, q_ref, k_ref, v_ref):
    x = x_ref[...].astype(BF16)
    rw_ref[...] = _dot(x, w_ref[:, :RW_COLS])
    cos = cos_ref[...]
    sin = sin_ref[...]
    lane = lax.broadcasted_iota(jnp.int32, cos.shape, 1)
    first_half = (lane & 32) == 0
    q = _dot(x, w_ref[:, RW_COLS:RW_COLS + DF_W])
    k = _dot(x, w_ref[:, RW_COLS + DF_W:RW_COLS + 2 * DF_W])
    for g in range(DF_W // LANES):
        sl = slice(g * LANES, (g + 1) * LANES)
        q_ref[:, sl] = _rope128(q[:, sl], cos, sin, first_half)
        k_ref[:, sl] = _rope128(k[:, sl], cos, sin, first_half)
    v_ref[...] = _dot(x, w_ref[:, RW_COLS + 2 * DF_W:])


def _inproj(x, w_bf, cos, sin):
    rows, d = x.shape
    in_cols = w_bf.shape[1]
    tm = ROW_TILE
    row = lambda i: (i, 0)
    return pl.pallas_call(
        _inproj_kernel,
        grid=(rows // tm,),
        in_specs=[pl.BlockSpec((tm, d), row),
                  pl.BlockSpec((d, in_cols), lambda i: (0, 0)),
                  pl.BlockSpec((tm, LANES), row),
                  pl.BlockSpec((tm, LANES), row)],
        out_specs=[pl.BlockSpec((tm, RW_COLS), row),
                   pl.BlockSpec((tm, DF_W), row),
                   pl.BlockSpec((tm, DF_W), row),
                   pl.BlockSpec((tm, DF_W), row)],
        out_shape=[jax.ShapeDtypeStruct((rows, RW_COLS), F32),
                   jax.ShapeDtypeStruct((rows, DF_W), F32),
                   jax.ShapeDtypeStruct((rows, DF_W), F32),
                   jax.ShapeDtypeStruct((rows, DF_W), F32)],
        compiler_params=_cparams(("parallel",)),
        name="inproj_rope",
    )(x, w_bf, cos, sin)


def _prep_kernel(t_real, cols_ref, shift0_ref, mu_ref, w0_ref, w2_ref, a0_ref, a2_ref, g2_ref,
                 kk_ref, ka_ref, headsum_ref,
                 r_ref, lw_ref, k_ref, v_ref, kkn_ref, b_ref, g_ref, carry_ref):
    ti = pl.program_id(1)
    tm = cols_ref.shape[1]

    @pl.when(ti == 0)
    def _():
        carry_ref[...] = jnp.broadcast_to(shift0_ref[0], carry_ref.shape)

    x = cols_ref[0]
    rowi = lax.broadcasted_iota(jnp.int32, x.shape, 0)
    prev = jnp.where(rowi == 0, carry_ref[0:1, :], pltpu.roll(x, 1, 0))
    carry_ref[...] = jnp.broadcast_to(x[tm - 1:tm, :], carry_ref.shape)
    mixed = x + (prev - x) * mu_ref[...]
    r = mixed[:, 0:RW_W]
    k = mixed[:, RW_W:2 * RW_W]
    v = mixed[:, 2 * RW_W:3 * RW_W]
    o = 3 * RW_W
    xw = mixed[:, o:o + DECAY_LORA]
    xa = mixed[:, o + DECAY_LORA:o + DECAY_LORA + AAA_LORA]
    xg = mixed[:, o + DECAY_LORA + AAA_LORA:]
    d = w0_ref[...] + _dot3(jnp.tanh(xw), w2_ref[...])
    lw = -jax.nn.sigmoid(d) * math.exp(-0.5)
    a = jax.nn.sigmoid(a0_ref[...] + _dot3(xa, a2_ref[...]))
    g = _dot3(jax.nn.sigmoid(xg), g2_ref[...])
    kk = k * kk_ref[...]
    k2 = k * (1.0 + (a - 1.0) * ka_ref[...])
    ssq = _dot3(kk * kk, headsum_ref[...])
    kkn = kk / jnp.maximum(jnp.sqrt(ssq), 1e-12)
    valid = (ti * tm + lax.broadcasted_iota(jnp.int32, (tm, RW_W), 0)) < t_real
    r_ref[0] = r
    lw_ref[0] = jnp.where(valid, lw, 0.0)
    k_ref[0] = jnp.where(valid, k2, 0.0)
    v_ref[0] = v
    kkn_ref[0] = jnp.where(valid, kkn, 0.0)
    b_ref[0] = jnp.where(valid, kkn * a, 0.0)
    g_ref[0] = g


def _rwkv_prep(cols, shift0, t_real, p):
    bx, tx, _ = cols.shape
    tm = min(PREP_TILE, tx)
    blk = lambda b, t: (b, t, 0)
    vec = lambda n: pl.BlockSpec((1, n), lambda b, t: (0, 0))
    mat = lambda r, c: pl.BlockSpec((r, c), lambda b, t: (0, 0))
    outs = [jax.ShapeDtypeStruct((bx, tx, RW_W), F32)] * 7
    return pl.pallas_call(
        functools.partial(_prep_kernel, t_real),
        grid=(bx, tx // tm),
        in_specs=[pl.BlockSpec((1, tm, RW_COLS), blk),
                  pl.BlockSpec((1, 1, RW_COLS), lambda b, t: (b, 0, 0)),
                  vec(RW_COLS), vec(RW_W), mat(DECAY_LORA, RW_W), vec(RW_W), mat(AAA_LORA, RW_W),
                  mat(GATE_LORA, RW_W), vec(RW_W), vec(RW_W), mat(RW_W, RW_W)],
        out_specs=[pl.BlockSpec((1, tm, RW_W), blk)] * 7,
        out_shape=outs,
        scratch_shapes=[pltpu.VMEM((SUBLANES, RW_COLS), F32)],
        compiler_params=_cparams(("parallel", "arbitrary")),
        name="rwkv_prep",
    )(cols, shift0.reshape(bx, 1, RW_COLS), p["mu_shift"], p["w0"], p["w2_decay"], p["a0"], p["a2"],
      p["g2"], p["k_k"], p["k_a"], p["headsum"])


def _wkv_kernel(r_ref, lw_ref, k_ref, v_ref, kkn_ref, b_ref, g_ref, s0_ref, rk_ref, lnw_ref, lnb_ref,
                out_ref, s_ref):
    ci = pl.program_id(1)
    c = r_ref.shape[1]

    @pl.when(ci == 0)
    def _():
        s_ref[...] = s0_ref[...]

    row = lax.broadcasted_iota(jnp.int32, (c, c), 0)
    col = lax.broadcasted_iota(jnp.int32, (c, c), 1)
    tri = (col <= row).astype(F32)
    nb = r_ref.shape[0]
    a_hat, b_hat, k_hat, r_hat, b_end, k_end, gamma, rkk, v = ([] for _ in range(9))
    for bi in range(nb):
        lw = lw_ref[bi]
        cum = _dot3(tri, lw)
        cum_last = cum[c - 1:c, :]
        g_out = jnp.exp(-cum)
        g_end = jnp.exp(cum_last - cum)
        a_hat.append(-kkn_ref[bi] * jnp.exp(cum - lw))
        b_hat.append(b_ref[bi] * g_out)
        k_hat.append(k_ref[bi] * g_out)
        r_hat.append(r_ref[bi] * jnp.exp(cum))
        b_end.append(b_ref[bi] * g_end)
        k_end.append(k_ref[bi] * g_end)
        gamma.append(jnp.exp(cum_last))
        rkk.append(r_ref[bi] * k_ref[bi] * rk_ref[...])
        v.append(v_ref[bi])

    row2 = lax.broadcasted_iota(jnp.int32, (2 * c, 2 * c), 0)
    col2 = lax.broadcasted_iota(jnp.int32, (2 * c, 2 * c), 1)
    rt = jnp.where(row2 >= c, row2 - c, row2)
    cs = jnp.where(col2 >= c, col2 - c, col2)
    keep = cs < rt + jnp.where(row2 >= c, 1, 0)

    eye = (row == col).astype(F32)
    n_sq = int(math.log2(c))
    probs = [(bi, h) for bi in range(nb) for h in range(RW_HEADS)]
    n = range(len(probs))
    hsl = [slice(h * RW_DH, (h + 1) * RW_DH) for _, h in probs]
    bf = lambda x: x.astype(BF16)
    s_old = [s_ref[bi, h] for bi, h in probs]
    a_h = [a_hat[bi][:, hsl[i]] for i, (bi, _) in enumerate(probs)]
    v_h = [v[bi][:, hsl[i]] for i, (bi, _) in enumerate(probs)]
    r_h = [r_hat[bi][:, hsl[i]] for i, (bi, _) in enumerate(probs)]
    vb = [bf(x) for x in v_h]
    prod = [jnp.where(keep, _dot(bf(jnp.concatenate([a_h[i], r_h[i]], axis=0)),
                                 bf(jnp.concatenate([b_hat[bi][:, hsl[i]], k_hat[bi][:, hsl[i]]], axis=0)),
                                 ((1,), (1,))), 0.0) for i, (bi, _) in enumerate(probs)]
    p = [prod[i][:c, :c] for i in n]
    t = [eye + p[i] for i in n]
    w = [_dot(bf(prod[i][:c, c:]), vb[i]) for i in n]
    for _ in range(n_sq - 1):
        pb = [bf(p[i]) for i in n]
        p = [_dot(pb[i], pb[i]) for i in n]
        t = [t[i] + _dot(bf(p[i]), bf(t[i])) for i in n]
    tb = [bf(t[i]) for i in n]
    u0 = [_dot(tb[i], bf(w[i])) for i in n]
    ta = [_dot(tb[i], bf(a_h[i])) for i in n]
    xs = [_dot(bf(jnp.concatenate([ta[i], r_h[i]], axis=0)), bf(s_old[i]), ((1,), (1,))) for i in n]
    uv = [bf(jnp.concatenate([xs[i][:c] + u0[i], v_h[i]], axis=0)) for i in n]
    s_new = [s_old[i] * gamma[bi][:, hsl[i]]
             + _dot(uv[i], bf(jnp.concatenate([b_end[bi][:, hsl[i]], k_end[bi][:, hsl[i]]], axis=0)),
                    ((0,), (0,))) for i, (bi, _) in enumerate(probs)]
    y = [xs[i][c:] + _dot(bf(prod[i][c:, :]), uv[i]) for i in n]
    for i, (bi, h) in enumerate(probs):
        s_ref[bi, h] = s_new[i]
    for i, (bi, h) in enumerate(probs):
        sl = hsl[i]
        mu = jnp.mean(y[i], axis=-1, keepdims=True)
        yc = y[i] - mu
        var = jnp.mean(yc * yc, axis=-1, keepdims=True)
        yn = yc * lax.rsqrt(var + LNX_EPS) * lnw_ref[:, sl] + lnb_ref[:, sl]
        bonus = jnp.sum(rkk[bi][:, sl], axis=-1, keepdims=True) * v_h[i]
        out_ref[bi, :, sl] = (yn + bonus) * g_ref[bi, :, sl]


def _wkv(prep, s0, p):
    r = prep[0]
    bx, tx, _ = r.shape
    c = WKV_CHUNK
    nb = WKV_BATCH
    blk = pl.BlockSpec((nb, c, RW_W), lambda b, t: (b, t, 0))
    vec = pl.BlockSpec((1, RW_W), lambda b, t: (0, 0))
    st = pl.BlockSpec((nb, RW_HEADS, RW_DH, RW_DH), lambda b, t: (b, 0, 0, 0))
    return pl.pallas_call(
        _wkv_kernel,
        grid=(bx // nb, tx // c),
        in_specs=[blk] * 7 + [st, vec, vec, vec],
        out_specs=[blk, st],
        out_shape=[jax.ShapeDtypeStruct((bx, tx, RW_W), F32),
                   jax.ShapeDtypeStruct((bx, RW_HEADS, RW_DH, RW_DH), F32)],
        compiler_params=_cparams(("parallel", "arbitrary")),
        name="wkv_chunk",
    )(*prep, s0, p["r_k"], p["lnx_w"], p["lnx_b"])


def _attn_kernel(lam_init, lam_ref, q_ref, k_ref, v_ref, sw_ref, o_ref, kt_ref, vx_ref, m_ref, acc_ref):
    qi = pl.program_id(2)
    tq = q_ref.shape[1]

    @pl.when(qi == 0)
    def _():
        kt_ref[...] = k_ref[0].T.astype(BF16)
        vx_ref[:, :DF_DV] = v_ref[0].astype(BF16)
        vx_ref[:, DF_DV:] = jnp.ones((vx_ref.shape[0], DF_DV), BF16)

    q = (q_ref[0] * (DF_DK ** -0.5)).astype(BF16)
    qc = [q[:, :DF_DK], q[:, DF_DK:]]
    m_ref[...] = jnp.full(m_ref.shape, -jnp.inf, F32)
    acc_ref[...] = jnp.zeros(acc_ref.shape, F32)
    rowi = lax.broadcasted_iota(jnp.int32, (tq, tq), 0)
    coli = lax.broadcasted_iota(jnp.int32, (tq, tq), 1)
    causal = coli <= rowi

    def block(kj, masked):
        start = pl.multiple_of(kj * tq, tq)
        vblk = vx_ref[pl.ds(start, tq), :]
        s = [_dot(qc[c], kt_ref[c * DF_DK:(c + 1) * DF_DK, pl.ds(start, tq)]) for c in range(2)]
        if masked:
            s = [jnp.where(causal, s[c], -jnp.inf) for c in range(2)]
        m_old = [m_ref[c] for c in range(2)]
        m_new = [jnp.maximum(m_old[c], jnp.max(s[c], axis=-1, keepdims=True)) for c in range(2)]
        pr = [jnp.exp(s[c] - m_new[c]).astype(BF16) for c in range(2)]
        pv = [_dot(pr[c], vblk) for c in range(2)]
        for c in range(2):
            acc_ref[c] = jnp.exp(m_old[c] - m_new[c]) * acc_ref[c] + pv[c]
            m_ref[c] = m_new[c]

    def body(kj, carry):
        block(kj, False)
        return carry

    lax.fori_loop(0, qi, body, 0)
    block(qi, True)
    o = (acc_ref[0, :, :DF_DV] / acc_ref[0, :, DF_DV:]
         - lam_ref[0] * (acc_ref[1, :, :DF_DV] / acc_ref[1, :, DF_DV:]))
    o = o * lax.rsqrt(jnp.mean(o * o, axis=-1, keepdims=True) + SUBLN_EPS)
    o_ref[0] = o * sw_ref[...] * (1.0 - lam_init)


def _attn_prompt(q, k, v, lam, subln_w, lam_init):
    b, tp, _ = q.shape
    tq = ATT_TILE
    return pl.pallas_call(
        functools.partial(_attn_kernel, lam_init),
        grid=(b, DF_HEADS, tp // tq),
        in_specs=[pl.BlockSpec(memory_space=pltpu.SMEM),
                  pl.BlockSpec((1, tq, DF_DV), lambda bi, h, i: (bi, i, h)),
                  pl.BlockSpec((1, tp, DF_DV), lambda bi, h, i: (bi, 0, h)),
                  pl.BlockSpec((1, tp, DF_DV), lambda bi, h, i: (bi, 0, h)),
                  pl.BlockSpec((1, DF_DV), lambda bi, h, i: (0, h))],
        out_specs=pl.BlockSpec((1, tq, DF_DV), lambda bi, h, i: (bi, i, h)),
        out_shape=jax.ShapeDtypeStruct((b, tp, DF_W), F32),
        scratch_shapes=[pltpu.VMEM((2 * DF_DK, tp), BF16), pltpu.VMEM((tp, 2 * DF_DV), BF16),
                        pltpu.VMEM((2, tq, 1), F32), pltpu.VMEM((2, tq, 2 * DF_DV), F32)],
        compiler_params=_cparams(("parallel", "parallel", "arbitrary")),
        name="diff_attn_prompt",
    )(lam, q, k, v, subln_w)


def _decode_kernel(lam_init, n_pg, pt_ref, lam_ref, q_ref, kn_ref, vn_ref, *refs):
    ck_refs = refs[:n_pg]
    cv_refs = refs[n_pg:2 * n_pg]
    sw_ref, o_ref, qrow_ref, m_ref, l_ref, acc_ref = refs[2 * n_pg:]
    pi = pl.program_id(1)
    n_pages = pl.num_programs(1)
    nq = 2 * DF_HEADS
    page = ck_refs[0].shape[2]

    @pl.when(pi == 0)
    def _():
        rowi = lax.broadcasted_iota(jnp.int32, (nq, DF_W), 0)
        coli = lax.broadcasted_iota(jnp.int32, (nq, DF_W), 1)
        qb = jnp.broadcast_to(q_ref[0] * (DF_DK ** -0.5), (nq, DF_W))
        qrow_ref[...] = jnp.where(jnp.right_shift(coli, int(math.log2(DF_DK))) == rowi, qb, 0.0)
        m_ref[...] = jnp.full(m_ref.shape, -jnp.inf, F32)
        l_ref[...] = jnp.zeros(l_ref.shape, F32)
        acc_ref[...] = jnp.zeros(acc_ref.shape, F32)

    qrow = qrow_ref[...]
    qb = qrow.astype(BF16)
    s = jnp.concatenate([_dot(qb, ck_refs[g][0].astype(BF16)) for g in range(n_pg)], axis=1)
    m_old = m_ref[...]
    m_new = jnp.maximum(m_old, jnp.max(s, axis=-1, keepdims=True))
    alpha = jnp.exp(m_old - m_new)
    pr = jnp.exp(s - m_new)
    l_ref[...] = alpha * l_ref[...] + jnp.sum(pr, axis=-1, keepdims=True)
    prb = pr.astype(BF16)
    for h in range(DF_HEADS):
        pv = None
        for g in range(n_pg):
            v_gh = cv_refs[g][0, pl.ds(h, page, stride=DF_HEADS), :].astype(BF16)
            d = _dot(prb[:, g * page:(g + 1) * page], v_gh)
            pv = d if pv is None else pv + d
        sl = slice(h * DF_DV, (h + 1) * DF_DV)
        acc_ref[:, sl] = alpha * acc_ref[:, sl] + pv
    m_ref[...] = m_new

    @pl.when(pi == n_pages - 1)
    def _():
        s_self = jnp.sum(qrow * kn_ref[0], axis=-1, keepdims=True)
        m_old = m_ref[...]
        m_fin = jnp.maximum(m_old, s_self)
        alpha = jnp.exp(m_old - m_fin)
        p_self = jnp.exp(s_self - m_fin)
        l_fin = alpha * l_ref[...] + p_self
        acc = (alpha * acc_ref[...] + p_self * vn_ref[0]) / l_fin
        for h in range(DF_HEADS):
            sl = slice(h * DF_DV, (h + 1) * DF_DV)
            o = acc[2 * h:2 * h + 1, sl] - lam_ref[0] * acc[2 * h + 1:2 * h + 2, sl]
            o = o * lax.rsqrt(jnp.mean(o * o, axis=-1, keepdims=True) + SUBLN_EPS)
            o_ref[0, :, sl] = o * sw_ref[:, sl] * (1.0 - lam_init)


def _attn_sample(q, k_new, v_new, cache_k, cache_v, page_table, lam, subln_w, lam_init):
    bs = q.shape[0]
    n_pages = page_table.shape[1]
    n_pool, page = cache_k.shape[:2]
    ck = jnp.transpose(cache_k, (0, 2, 3, 4, 1)).reshape(n_pool, DF_W, page)
    cv = cache_v.reshape(n_pool, page * DF_HEADS, DF_DV)
    n_pg = math.gcd(DEC_PAGES, n_pages)
    row = pl.BlockSpec((1, 1, DF_W), lambda b, p, pt: (b, 0, 0))
    ck_specs = [pl.BlockSpec((1, DF_W, page), lambda b, p, pt, g=g: (pt[b, p * n_pg + g], 0, 0))
                for g in range(n_pg)]
    cv_specs = [pl.BlockSpec((1, page * DF_HEADS, DF_DV), lambda b, p, pt, g=g: (pt[b, p * n_pg + g], 0, 0))
                for g in range(n_pg)]
    nq = 2 * DF_HEADS
    grid_spec = pltpu.PrefetchScalarGridSpec(
        num_scalar_prefetch=1,
        grid=(bs, n_pages // n_pg),
        in_specs=[pl.BlockSpec(memory_space=pltpu.SMEM), row, row, row] + ck_specs + cv_specs
                 + [pl.BlockSpec((1, DF_W), lambda b, p, pt: (0, 0))],
        out_specs=row,
        scratch_shapes=[pltpu.VMEM((nq, DF_W), F32), pltpu.VMEM((nq, 1), F32),
                        pltpu.VMEM((nq, 1), F32), pltpu.VMEM((nq, DF_W), F32)],
    )
    out = pl.pallas_call(
        functools.partial(_decode_kernel, lam_init, n_pg),
        grid_spec=grid_spec,
        out_shape=jax.ShapeDtypeStruct((bs, 1, DF_W), F32),
        compiler_params=_cparams(("parallel", "arbitrary")),
        name="diff_attn_paged",
    )(page_table, lam, q.reshape(bs, 1, DF_W), k_new.reshape(bs, 1, DF_W), v_new.reshape(bs, 1, DF_W),
      *([ck] * n_pg), *([cv] * n_pg), subln_w)
    return out.reshape(bs, DF_W)


def _layer_norm(x, w, b):
    mu = jnp.mean(x, axis=-1, keepdims=True)
    xc = x - mu
    var = jnp.mean(xc * xc, axis=-1, keepdims=True)
    return xc * lax.rsqrt(var + LN_EPS) * w + b


def _outproj_kernel(x_ref, rw_ref, o_ref, w_ref, lnw_ref, lnb_ref, y_ref, yt_ref):
    mix = _dot(rw_ref[...].astype(BF16), w_ref[:RW_W, :]) + _dot(o_ref[...].astype(BF16), w_ref[RW_W:, :])
    y = _layer_norm(DEEP_ALPHA * x_ref[...] + mix, lnw_ref[...], lnb_ref[...])
    y_ref[...] = y
    yt_ref[...] = y.T.astype(BF16)


def _outproj(x, rw_out, o, w_bf, ln_w, ln_b):
    rows, d = x.shape
    tm = ROW_TILE
    row = lambda i: (i, 0)
    return pl.pallas_call(
        _outproj_kernel,
        grid=(rows // tm,),
        in_specs=[pl.BlockSpec((tm, d), row), pl.BlockSpec((tm, RW_W), row), pl.BlockSpec((tm, DF_W), row),
                  pl.BlockSpec(w_bf.shape, lambda i: (0, 0)),
                  pl.BlockSpec((1, d), lambda i: (0, 0)), pl.BlockSpec((1, d), lambda i: (0, 0))],
        out_specs=[pl.BlockSpec((tm, d), row), pl.BlockSpec((d, tm), lambda i: (0, i))],
        out_shape=[jax.ShapeDtypeStruct((rows, d), F32), jax.ShapeDtypeStruct((d, rows), BF16)],
        compiler_params=_cparams(("parallel",)),
        name="outproj_ln",
    )(x, rw_out, o, w_bf, ln_w, ln_b)


def _peer_pairs():
    n = PK_TOPK + 1
    return [(i, j) for i in range(n) for j in range(n) if (i + 1) * (j + 1) <= n]


def _extract_top(cur, n, n_pad):
    vals = []
    rowi = lax.broadcasted_iota(jnp.int32, (n_pad, cur.shape[1]), 0)
    stacked = jnp.zeros((n_pad, cur.shape[1]), F32)
    for i in range(n):
        m = jnp.max(cur, axis=0, keepdims=True)
        vals.append(m)
        stacked = jnp.where(rowi == i, m, stacked)
        cur = jnp.where(cur == m, -jnp.inf, cur)
    return vals, stacked


def _peer_score_kernel(x_ref, wq_ref, sk_ref, sela_ref, selb_ref,
                       e1_ref, e2_ref, psi_ref, qt_ref):
    n_top = PK_TOPK + 1
    half = PK_DQ // 2
    tt = x_ref.shape[0]
    qt_ref[...] = _dot3(wq_ref[...], x_ref[...], ((1,), (1,)))
    n_cand = sela_ref.shape[0]
    n_pad = sela_ref.shape[1]
    cand_row = lax.broadcasted_iota(jnp.int32, (n_cand, tt), 0)
    n_pairs = len(_peer_pairs())
    for h in range(PK_HEADS):
        s1 = _dot3(sk_ref[h, 0], qt_ref[pl.ds(h * PK_DQ, half), :])
        s2 = _dot3(sk_ref[h, 1], qt_ref[pl.ds(h * PK_DQ + half, half), :])
        top1, a_st = _extract_top(s1, n_top, n_pad)
        top2, b_st = _extract_top(s2, n_top, n_pad)
        cand = _dot(sela_ref[...], a_st, precision=HI) + _dot(selb_ref[...], b_st, precision=HI)
        cand = jnp.where(cand_row < n_pairs, cand, -jnp.inf)
        ctop, _ = _extract_top(cand, n_top, n_pad)
        cmax = ctop[0]
        theta = 0.5 * (ctop[PK_TOPK - 1] + ctop[PK_TOPK])
        z = jnp.sum(jnp.where(cand >= theta, jnp.exp(cand - cmax), 0.0), axis=0, keepdims=True)
        e1_ref[h] = jnp.exp(s1 - top1[0]) * (0.5 / z)
        e2_ref[h] = jnp.exp(s2 - top2[0])
        psi_ref[h] = jnp.exp(theta - s1 - top2[0])


def _peer_scores(x1, wq_t, subkeys, sel_a, sel_b):
    rows, d = x1.shape
    tt = PEER_TOK
    out = jax.ShapeDtypeStruct((PK_HEADS, N_KEYS, rows), F32)
    ospec = pl.BlockSpec((PK_HEADS, N_KEYS, tt), lambda i: (0, 0, i))
    return pl.pallas_call(
        _peer_score_kernel,
        grid=(rows // tt,),
        in_specs=[pl.BlockSpec((tt, d), lambda i: (i, 0)),
                  pl.BlockSpec(wq_t.shape, lambda i: (0, 0)),
                  pl.BlockSpec(subkeys.shape, lambda i: (0, 0, 0, 0)),
                  pl.BlockSpec(sel_a.shape, lambda i: (0, 0)),
                  pl.BlockSpec(sel_b.shape, lambda i: (0, 0))],
        out_specs=[ospec] * 3,
        out_shape=[out] * 3,
        scratch_shapes=[pltpu.VMEM((PK_HEADS * PK_DQ, tt), F32)],
        compiler_params=_cparams(("parallel",)),
        name="peer_scores",
    )(x1, wq_t, subkeys, sel_a, sel_b)


def _peer_expert_kernel(x_ref, xt_ref, u_ref, vt_ref, e1_ref, e2_ref, psi_ref, lnw_ref, lnb_ref,
                        y_ref, wt_ref, acc_ref, act_ref, pb_ref, eb_ref):
    ei = pl.program_id(1)
    te = u_ref.shape[0]
    tt = x_ref.shape[0]

    @pl.when(ei == 0)
    def _():
        acc_ref[...] = jnp.zeros(acc_ref.shape, F32)

    n_rows = te // N_KEYS
    k_rows = PEER_KSPLIT // N_KEYS
    n_sub = N_KEYS // SUBLANES
    for j in range(n_rows):
        i1 = ei * n_rows + j
        for h in range(PK_HEADS):
            pb_ref[j, h] = jnp.broadcast_to(psi_ref[h, pl.ds(i1, 1), :], (SUBLANES, tt))
            eb_ref[j, h] = jnp.broadcast_to(e1_ref[h, pl.ds(i1, 1), :], (SUBLANES, tt))
    part = None
    for j in range(n_rows):
        rows = slice(j * N_KEYS, (j + 1) * N_KEYS)
        if j % k_rows == 0:
            slot = (j // k_rows) % 2
            act_ref[slot] = _dot(u_ref[j * N_KEYS:(j + k_rows) * N_KEYS, :], xt_ref[...])
        arow = slice((j % k_rows) * N_KEYS, (j % k_rows + 1) * N_KEYS)
        for tg in range(tt // LANES):
            ls = slice(tg * LANES, (tg + 1) * LANES)
            gate = jnp.zeros((n_sub, SUBLANES, LANES), F32)
            for h in range(PK_HEADS):
                e2 = e2_ref[h, :, ls].reshape(n_sub, SUBLANES, LANES)
                gate = gate + jnp.where(e2 >= pb_ref[j, h, :, ls][None], eb_ref[j, h, :, ls][None] * e2, 0.0)
            a = act_ref[slot, arow, ls]
            gate = gate.reshape(N_KEYS, LANES)
            wt_ref[rows, ls] = (gate * (a + a * lax.erf(a * (2.0 ** -0.5)))).astype(BF16)
        if (j + 1) % k_rows == 0:
            ks = slice((j + 1 - k_rows) * N_KEYS, (j + 1) * N_KEYS)
            d = _dot(vt_ref[:, ks], wt_ref[ks, :])
            part = d if part is None else part + d
    acc_ref[...] += part

    @pl.when(ei == pl.num_programs(1) - 1)
    def _():
        ch = acc_ref[...].T
        y_ref[...] = _layer_norm(DEEP_ALPHA * x_ref[...] + ch, lnw_ref[...], lnb_ref[...])


def _peer_experts(x1, x1_t, u_bf, vt_bf, e1, e2, psi, ln_w, ln_b):
    rows, d = x1.shape
    n_exp = u_bf.shape[0]
    tt = PEER_TOK
    te = PEER_EXP
    tok = lambda i, e: (i, 0)
    sc = pl.BlockSpec((PK_HEADS, N_KEYS, tt), lambda i, e: (0, 0, i))
    vec = pl.BlockSpec((1, d), lambda i, e: (0, 0))
    return pl.pallas_call(
        _peer_expert_kernel,
        grid=(rows // tt, n_exp // te),
        in_specs=[pl.BlockSpec((tt, d), tok), pl.BlockSpec((d, tt), lambda i, e: (0, i)),
                  pl.BlockSpec((te, d), lambda i, e: (e, 0)),
                  pl.BlockSpec((d, te), lambda i, e: (0, e)),
                  sc, sc, sc, vec, vec],
        out_specs=pl.BlockSpec((tt, d), tok),
        out_shape=jax.ShapeDtypeStruct((rows, d), F32),
        scratch_shapes=[pltpu.VMEM((te, tt), BF16), pltpu.VMEM((d, tt), F32),
                        pltpu.VMEM((2, PEER_KSPLIT, tt), F32),
                        pltpu.VMEM((te // N_KEYS, PK_HEADS, SUBLANES, tt), F32),
                        pltpu.VMEM((te // N_KEYS, PK_HEADS, SUBLANES, tt), F32)],
        compiler_params=_cparams(("parallel", "arbitrary")),
        name="peer_experts",
    )(x1, x1_t, u_bf, vt_bf, e1, e2, psi, ln_w, ln_b)


def _rope_tables(pos):
    half = DF_DK // 2
    inv = ROPE_THETA ** (-jnp.arange(half, dtype=F32) / half)
    ang = pos.astype(F32)[:, None] * inv[None, :]
    cos, sin = jnp.cos(ang), jnp.sin(ang)
    cos128 = jnp.concatenate([cos, cos, cos, cos], axis=1)
    sin128 = jnp.concatenate([-sin, sin, -sin, sin], axis=1)
    return cos128, sin128


def _round_up(n, m):
    return -(-n // m) * m


def kernel(x_prompt, x_sample, cache_k, cache_v, page_table, state_wkv, state_shift, meta_tokens, w_in,
           mu_shift, w0, w2_decay, a0, a2, g2, k_k, k_a, r_k, lnx_w, lnx_b, lam_q1, lam_k1, lam_q2, lam_k2,
           subln_w, w_out, ln1_w, ln1_b, pk_wq, pk_subkeys, peer_u, peer_v, ln2_w, ln2_b):
    assert w_in.shape[0] == DEPTH == 1
    bp, seq, d = x_prompt.shape
    bs, dec_seq, _ = x_sample.shape
    assert dec_seq == 1
    t_prompt = N_META + seq
    tp = _round_up(t_prompt, math.lcm(ATT_TILE, PREP_TILE, WKV_CHUNK))
    n_prompt_rows = bp * tp
    rows = _round_up(n_prompt_rows + bs, ROW_TILE)
    page = cache_k.shape[2]
    past = page_table.shape[1] * page
    lam_init = 0.8 - 0.6 * math.exp(-0.3 * 0)
    row2 = lambda a: a.reshape(1, -1)

    meta = jnp.broadcast_to(meta_tokens[None].astype(x_prompt.dtype), (bp, N_META, d))
    xp = jnp.concatenate([meta, x_prompt, jnp.zeros((bp, tp - t_prompt, d), x_prompt.dtype)], axis=1)
    x_all = jnp.concatenate([xp.reshape(n_prompt_rows, d), x_sample.reshape(bs, d),
                             jnp.zeros((rows - n_prompt_rows - bs, d), x_prompt.dtype)], axis=0)
    pos = jnp.concatenate([jnp.tile(jnp.arange(tp), bp), jnp.full((rows - n_prompt_rows,), past)])
    cos, sin = _rope_tables(pos)

    rw_cols, q, k, v = _inproj(x_all, w_in[0].astype(BF16), cos, sin)

    pr = lambda a: a[:n_prompt_rows].reshape(bp, tp, -1)
    sm = lambda a: a[n_prompt_rows:n_prompt_rows + bs]

    hs = jnp.arange(RW_W) // RW_DH
    rw_params = {
        "mu_shift": row2(mu_shift[0]), "w0": row2(w0[0]), "w2_decay": w2_decay[0], "a0": row2(a0[0]),
        "a2": a2[0], "g2": g2[0], "k_k": row2(k_k[0]), "k_a": row2(k_a[0]),
        "headsum": (hs[:, None] == hs[None, :]).astype(F32),
        "r_k": row2(r_k[0]), "lnx_w": row2(lnx_w[0]), "lnx_b": row2(lnx_b[0]),
    }
    rw_p = pr(rw_cols)
    prep_p = _rwkv_prep(rw_p, jnp.zeros((bp, RW_COLS), F32), t_prompt, rw_params)
    rw_out_p, wkv_p = _wkv(prep_p, jnp.zeros((bp, RW_HEADS, RW_DH, RW_DH), F32), rw_params)
    ts = WKV_CHUNK
    rw_s = jnp.concatenate([sm(rw_cols)[:, None, :], jnp.zeros((bs, ts - 1, RW_COLS), F32)], axis=1)
    prep_s = _rwkv_prep(rw_s, state_shift[0], dec_seq, rw_params)
    rw_out_s, wkv_s = _wkv(prep_s, state_wkv[0], rw_params)

    f32 = lambda t: t.astype(F32)
    lam = (jnp.exp(jnp.sum(f32(lam_q1[0]) * f32(lam_k1[0])))
           - jnp.exp(jnp.sum(f32(lam_q2[0]) * f32(lam_k2[0]))) + lam_init).reshape(1)
    sw = row2(subln_w[0])
    o_p = _attn_prompt(pr(q), pr(k), pr(v), lam, sw, lam_init)
    o_s = _attn_sample(sm(q), sm(k), sm(v), cache_k[0], cache_v[0], page_table, lam, sw, lam_init)

    tail = jnp.zeros((rows - n_prompt_rows - bs, RW_W), F32)
    rw_out = jnp.concatenate([rw_out_p.reshape(n_prompt_rows, RW_W), rw_out_s[:, 0, :], tail], axis=0)
    o_all = jnp.concatenate([o_p.reshape(n_prompt_rows, DF_W), o_s, tail], axis=0)
    x1, x1_t = _outproj(x_all, rw_out, o_all, w_out[0].astype(BF16), row2(ln1_w[0]), row2(ln1_b[0]))

    pairs = _peer_pairs()
    n_cand = _round_up(len(pairs), SUBLANES)
    n_pad = _round_up(PK_TOPK + 1, SUBLANES)
    ia = jnp.array([p[0] for p in pairs] + [0] * (n_cand - len(pairs)))
    ib = jnp.array([p[1] for p in pairs] + [0] * (n_cand - len(pairs)))
    sel_a = (ia[:, None] == jnp.arange(n_pad)[None, :]).astype(F32)
    sel_b = (ib[:, None] == jnp.arange(n_pad)[None, :]).astype(F32)
    e1, e2, psi = _peer_scores(x1, pk_wq[0].T, pk_subkeys[0], sel_a, sel_b)
    y = _peer_experts(x1, x1_t, peer_u[0].astype(BF16), peer_v[0].T.astype(BF16),
                      e1, e2, psi, row2(ln2_w[0]), row2(ln2_b[0]))

    y_prompt = pr(y)[:, N_META:t_prompt]
    y_sample = sm(y).reshape(bs, dec_seq, d)
    k_p = pr(k)[:, :t_prompt].reshape(1, bp, t_prompt, DF_HEADS, 2, DF_DK)
    v_p = pr(v)[:, :t_prompt].reshape(1, bp, t_prompt, DF_HEADS, DF_DV)
    shift_p = rw_p[:, t_prompt - 1][None]
    k_s = sm(k).reshape(1, bs, dec_seq, DF_HEADS, 2, DF_DK)
    v_s = sm(v).reshape(1, bs, dec_seq, DF_HEADS, DF_DV)
    shift_s = sm(rw_cols)[None]
    return (y_prompt, y_sample, k_p, v_p, wkv_p[None], shift_p, k_s, v_s, wkv_s[None], shift_s)
```

```python
import functools
import math

import jax
import jax.numpy as jnp
from jax import lax
from jax.experimental import pallas as pl
from jax.experimental.pallas import tpu as pltpu

F32 = jnp.float32
BF16 = jnp.bfloat16

N_META = 16
RW_HEADS = 8
RW_DH = 64
RW_W = RW_HEADS * RW_DH
DECAY_LORA = 64
AAA_LORA = 64
GATE_LORA = 128
RW_COLS = 3 * RW_W + DECAY_LORA + AAA_LORA + GATE_LORA
LNX_EPS = 64e-5
DF_HEADS = 4
DF_DK = 64
DF_DV = 2 * DF_DK
DF_W = DF_HEADS * DF_DV
ROPE_THETA = 10000.0
SUBLN_EPS = 1e-5
LN_EPS = 1e-5
PK_HEADS = 8
N_KEYS = 128
PK_DQ = 128
PK_TOPK = 16
DEPTH = 1
DEEP_ALPHA = (2 * DEPTH) ** 0.25

LANES = 128
SUBLANES = 8
VMEM_LIMIT = 56 * 1024 * 1024

ROW_TILE = 512
PREP_TILE = 128
WKV_CHUNK = 64
WKV_BATCH = 4
ATT_TILE = 384
DEC_PAGES = 8
PEER_TOK = 512
PEER_EXP = 2048
PEER_KSPLIT = 256
HI = lax.Precision.HIGHEST


def _cparams(sem):
    return pltpu.CompilerParams(dimension_semantics=sem, vmem_limit_bytes=VMEM_LIMIT)


def _dot(a, b, dims=((1,), (0,)), precision=None):
    return lax.dot_general(a, b, (dims, ((), ())), precision=precision,
                           preferred_element_type=F32)


def _dot3(a, b, dims=((1,), (0,))):
    a_hi = a.astype(BF16)
    a_lo = (a - a_hi.astype(F32)).astype(BF16)
    b_hi = b.astype(BF16)
    b_lo = (b - b_hi.astype(F32)).astype(BF16)
    return _dot(a_hi, b_hi, dims) + (_dot(a_lo, b_hi, dims) + _dot(a_hi, b_lo, dims))


def _rope128(z, cos, sin_signed, first_half):
    partner = jnp.where(first_half, pltpu.roll(z, LANES - 32, 1), pltpu.roll(z, 32, 1))
    return z * cos + partner * sin_signed


def _inproj_kernel(x_ref, w_ref, cos_ref, sin_ref, rw_ref, q_ref, k_ref, v_ref):
    x = x_ref[...].astype(BF16)
    rw_ref[...] = _dot(x, w_ref[:, :RW_COLS])
    cos = cos_ref[...]
    sin = sin_ref[...]
    lane = lax.broadcasted_iota(jnp.int32, cos.shape, 1)
    first_half = (lane & 32) == 0
    q = _dot(x, w_ref[:, RW_COLS:RW_COLS + DF_W])
    k = _dot(x, w_ref[:, RW_COLS + DF_W:RW_COLS + 2 * DF_W])
    for g in range(DF_W // LANES):
        sl = slice(g * LANES, (g + 1) * LANES)
        q_ref[:, sl] = _rope128(q[:, sl], cos, sin, first_half)
        k_ref[:, sl] = _rope128(k[:, sl], cos, sin, first_half)
    v_ref[...] = _dot(x, w_ref[:, RW_COLS + 2 * DF_W:])


def _inproj(x, w_bf, cos, sin):
    rows, d = x.shape
    in_cols = w_bf.shape[1]
    tm = ROW_TILE
    row = lambda i: (i, 0)
    return pl.pallas_call(
        _inproj_kernel,
        grid=(rows // tm,),
        in_specs=[pl.BlockSpec((tm, d), row),
                  pl.BlockSpec((d, in_cols), lambda i: (0, 0)),
                  pl.BlockSpec((tm, LANES), row),
                  pl.BlockSpec((tm, LANES), row)],
        out_specs=[pl.BlockSpec((tm, RW_COLS), row),
                   pl.BlockSpec((tm, DF_W), row),
                   pl.BlockSpec((tm, DF_W), row),
                   pl.BlockSpec((tm, DF_W), row)],
        out_shape=[jax.ShapeDtypeStruct((rows, RW_COLS), F32),
                   jax.ShapeDtypeStruct((rows, DF_W), F32),
                   jax.ShapeDtypeStruct((rows, DF_W), F32),
                   jax.ShapeDtypeStruct((rows, DF_W), F32)],
        compiler_params=_cparams(("parallel",)),
        name="inproj_rope",
    )(x, w_bf, cos, sin)


def _prep_kernel(t_real, cols_ref, shift0_ref, mu_ref, w0_ref, w2_ref, a0_ref, a2_ref, g2_ref,
                 kk_ref, ka_ref, headsum_ref,
                 r_ref, lw_ref, k_ref, v_ref, kkn_ref, b_ref, g_ref, carry_ref):
    ti = pl.program_id(1)
    tm = cols_ref.shape[0]

    @pl.when(ti == 0)
    def _():
        carry_ref[...] = jnp.broadcast_to(shift0_ref[0], carry_ref.shape)

    x = cols_ref[...]
    rowi = lax.broadcasted_iota(jnp.int32, x.shape, 0)
    prev = jnp.where(rowi == 0, carry_ref[0:1, :], pltpu.roll(x, 1, 0))
    carry_ref[...] = jnp.broadcast_to(x[tm - 1:tm, :], carry_ref.shape)
    mixed = x + (prev - x) * mu_ref[...]
    r = mixed[:, 0:RW_W]
    k = mixed[:, RW_W:2 * RW_W]
    v = mixed[:, 2 * RW_W:3 * RW_W]
    o = 3 * RW_W
    xw = mixed[:, o:o + DECAY_LORA]
    xa = mixed[:, o + DECAY_LORA:o + DECAY_LORA + AAA_LORA]
    xg = mixed[:, o + DECAY_LORA + AAA_LORA:]
    d = w0_ref[...] + _dot3(jnp.tanh(xw), w2_ref[...])
    lw = -jax.nn.sigmoid(d) * math.exp(-0.5)
    a = jax.nn.sigmoid(a0_ref[...] + _dot3(xa, a2_ref[...]))
    g = _dot3(jax.nn.sigmoid(xg), g2_ref[...])
    kk = k * kk_ref[...]
    k2 = k * (1.0 + (a - 1.0) * ka_ref[...])
    ssq = _dot3(kk * kk, headsum_ref[...])
    kkn = kk / jnp.maximum(jnp.sqrt(ssq), 1e-12)
    valid = (ti * tm + lax.broadcasted_iota(jnp.int32, (tm, RW_W), 0)) < t_real
    r_ref[0] = r
    lw_ref[0] = jnp.where(valid, lw, 0.0)
    k_ref[0] = jnp.where(valid, k2, 0.0)
    v_ref[0] = v
    kkn_ref[0] = jnp.where(valid, kkn, 0.0)
    b_ref[0] = jnp.where(valid, kkn * a, 0.0)
    g_ref[0] = g


def _rwkv_prep(cols, bx, tx, shift0, t_real, p):
    tm = min(PREP_TILE, tx)
    blk = lambda b, t: (b, t, 0)
    vec = lambda n: pl.BlockSpec((1, n), lambda b, t: (0, 0))
    mat = lambda r, c: pl.BlockSpec((r, c), lambda b, t: (0, 0))
    outs = [jax.ShapeDtypeStruct((bx, tx, RW_W), F32)] * 7
    return pl.pallas_call(
        functools.partial(_prep_kernel, t_real),
        grid=(bx, tx // tm),
        in_specs=[pl.BlockSpec((tm, RW_COLS), lambda b, t: (b * (tx // tm) + t, 0)),
                  pl.BlockSpec((1, 1, RW_COLS), lambda b, t: (b, 0, 0)),
                  vec(RW_COLS), vec(RW_W), mat(DECAY_LORA, RW_W), vec(RW_W), mat(AAA_LORA, RW_W),
                  mat(GATE_LORA, RW_W), vec(RW_W), vec(RW_W), mat(RW_W, RW_W)],
        out_specs=[pl.BlockSpec((1, tm, RW_W), blk)] * 7,
        out_shape=outs,
        scratch_shapes=[pltpu.VMEM((SUBLANES, RW_COLS), F32)],
        compiler_params=_cparams(("parallel", "arbitrary")),
        name="rwkv_prep",
    )(cols, shift0.reshape(bx, 1, RW_COLS), p["mu_shift"], p["w0"], p["w2_decay"], p["a0"], p["a2"],
      p["g2"], p["k_k"], p["k_a"], p["headsum"])


def _wkv_kernel(r_ref, lw_ref, k_ref, v_ref, kkn_ref, b_ref, g_ref, s0_ref, rk_ref, lnw_ref, lnb_ref,
                out_ref, s_ref):
    ci = pl.program_id(1)
    c = r_ref.shape[1]

    @pl.when(ci == 0)
    def _():
        s_ref[...] = s0_ref[...]

    row = lax.broadcasted_iota(jnp.int32, (c, c), 0)
    col = lax.broadcasted_iota(jnp.int32, (c, c), 1)
    tri = (col <= row).astype(F32)
    nb = r_ref.shape[0]
    a_hat, b_hat, k_hat, r_hat, b_end, k_end, gamma, rkk, v = ([] for _ in range(9))
    for bi in range(nb):
        lw = lw_ref[bi]
        cum = _dot3(tri, lw)
        cum_last = cum[c - 1:c, :]
        g_out = jnp.exp(-cum)
        g_end = jnp.exp(cum_last - cum)
        a_hat.append(-kkn_ref[bi] * jnp.exp(cum - lw))
        b_hat.append(b_ref[bi] * g_out)
        k_hat.append(k_ref[bi] * g_out)
        r_hat.append(r_ref[bi] * jnp.exp(cum))
        b_end.append(b_ref[bi] * g_end)
        k_end.append(k_ref[bi] * g_end)
        gamma.append(jnp.exp(cum_last))
        rkk.append(r_ref[bi] * k_ref[bi] * rk_ref[...])
        v.append(v_ref[bi])

    row2 = lax.broadcasted_iota(jnp.int32, (2 * c, 2 * c), 0)
    col2 = lax.broadcasted_iota(jnp.int32, (2 * c, 2 * c), 1)
    rt = jnp.where(row2 >= c, row2 - c, row2)
    cs = jnp.where(col2 >= c, col2 - c, col2)
    keep = cs < rt + jnp.where(row2 >= c, 1, 0)

    eye = (row == col).astype(F32)
    n_sq = int(math.log2(c))
    probs = [(bi, h) for bi in range(nb) for h in range(RW_HEADS)]
    n = range(len(probs))
    hsl = [slice(h * RW_DH, (h + 1) * RW_DH) for _, h in probs]
    bf = lambda x: x.astype(BF16)
    s_old = [s_ref[bi, h] for bi, h in probs]
    a_h = [a_hat[bi][:, hsl[i]] for i, (bi, _) in enumerate(probs)]
    v_h = [v[bi][:, hsl[i]] for i, (bi, _) in enumerate(probs)]
    r_h = [r_hat[bi][:, hsl[i]] for i, (bi, _) in enumerate(probs)]
    vb = [bf(x) for x in v_h]
    prod = [jnp.where(keep, _dot(bf(jnp.concatenate([a_h[i], r_h[i]], axis=0)),
                                 bf(jnp.concatenate([b_hat[bi][:, hsl[i]], k_hat[bi][:, hsl[i]]], axis=0)),
                                 ((1,), (1,))), 0.0) for i, (bi, _) in enumerate(probs)]
    p = [prod[i][:c, :c] for i in n]
    t = [eye + p[i] for i in n]
    w = [_dot(bf(prod[i][:c, c:]), vb[i]) for i in n]
    for _ in range(n_sq - 1):
        pb = [bf(p[i]) for i in n]
        p = [_dot(pb[i], pb[i]) for i in n]
        t = [t[i] + _dot(bf(p[i]), bf(t[i])) for i in n]
    tb = [bf(t[i]) for i in n]
    u0 = [_dot(tb[i], bf(w[i])) for i in n]
    ta = [_dot(tb[i], bf(a_h[i])) for i in n]
    xs = [_dot(bf(jnp.concatenate([ta[i], r_h[i]], axis=0)), bf(s_old[i]), ((1,), (1,))) for i in n]
    uv = [bf(jnp.concatenate([xs[i][:c] + u0[i], v_h[i]], axis=0)) for i in n]
    s_new = [s_old[i] * gamma[bi][:, hsl[i]]
             + _dot(uv[i], bf(jnp.concatenate([b_end[bi][:, hsl[i]], k_end[bi][:, hsl[i]]], axis=0)),
                    ((0,), (0,))) for i, (bi, _) in enumerate(probs)]
    y = [xs[i][c:] + _dot(bf(prod[i][c:, :]), uv[i]) for i in n]
    for i, (bi, h) in enumerate(probs):
        s_ref[bi, h] = s_new[i]
    for i, (bi, h) in enumerate(probs):
        sl = hsl[i]
        mu = jnp.mean(y[i], axis=-1, keepdims=True)
        yc = y[i] - mu
        var = jnp.mean(yc * yc, axis=-1, keepdims=True)
        yn = yc * lax.rsqrt(var + LNX_EPS) * lnw_ref[:, sl] + lnb_ref[:, sl]
        bonus = jnp.sum(rkk[bi][:, sl], axis=-1, keepdims=True) * v_h[i]
        out_ref[bi, :, sl] = (yn + bonus) * g_ref[bi, :, sl]


def _wkv(prep, s0, p):
    r = prep[0]
    bx, tx, _ = r.shape
    c = WKV_CHUNK
    nb = WKV_BATCH
    blk = pl.BlockSpec((nb, c, RW_W), lambda b, t: (b, t, 0))
    vec = pl.BlockSpec((1, RW_W), lambda b, t: (0, 0))
    st = pl.BlockSpec((nb, RW_HEADS, RW_DH, RW_DH), lambda b, t: (b, 0, 0, 0))
    return pl.pallas_call(
        _wkv_kernel,
        grid=(bx // nb, tx // c),
        in_specs=[blk] * 7 + [st, vec, vec, vec],
        out_specs=[blk, st],
        out_shape=[jax.ShapeDtypeStruct((bx, tx, RW_W), F32),
                   jax.ShapeDtypeStruct((bx, RW_HEADS, RW_DH, RW_DH), F32)],
        compiler_params=_cparams(("parallel", "arbitrary")),
        name="wkv_chunk",
    )(*prep, s0, p["r_k"], p["lnx_w"], p["lnx_b"])


def _attn_kernel(lam_init, lam_ref, q_ref, k_ref, v_ref, sw_ref, o_ref, kt_ref, vx_ref, m_ref, acc_ref):
    qi = pl.program_id(2)
    tq = q_ref.shape[0]

    @pl.when(qi == 0)
    def _():
        kt_ref[...] = k_ref[...].T.astype(BF16)
        vx_ref[:, :DF_DV] = v_ref[...].astype(BF16)
        vx_ref[:, DF_DV:] = jnp.ones((vx_ref.shape[0], DF_DV), BF16)

    q = (q_ref[...] * (DF_DK ** -0.5)).astype(BF16)
    qc = [q[:, :DF_DK], q[:, DF_DK:]]
    m_ref[...] = jnp.full(m_ref.shape, -jnp.inf, F32)
    acc_ref[...] = jnp.zeros(acc_ref.shape, F32)
    rowi = lax.broadcasted_iota(jnp.int32, (tq, tq), 0)
    coli = lax.broadcasted_iota(jnp.int32, (tq, tq), 1)
    causal = coli <= rowi

    def block(start, width, masked):
        vblk = vx_ref[pl.ds(start, width), :]
        s = [_dot(qc[c], kt_ref[c * DF_DK:(c + 1) * DF_DK, pl.ds(start, width)]) for c in range(2)]
        if masked:
            s = [jnp.where(causal, s[c], -jnp.inf) for c in range(2)]
        m_old = [m_ref[c] for c in range(2)]
        m_new = [jnp.maximum(m_old[c], jnp.max(s[c], axis=-1, keepdims=True)) for c in range(2)]
        pr = [jnp.exp(s[c] - m_new[c]).astype(BF16) for c in range(2)]
        pv = [_dot(pr[c], vblk) for c in range(2)]
        for c in range(2):
            acc_ref[c] = jnp.exp(m_old[c] - m_new[c]) * acc_ref[c] + pv[c]
            m_ref[c] = m_new[c]

    def pair(pj, carry):
        block(pl.multiple_of(pj * (2 * tq), 2 * tq), 2 * tq, False)
        return carry

    lax.fori_loop(0, qi // 2, pair, 0)

    @pl.when(qi % 2 == 1)
    def _():
        block(pl.multiple_of((qi - 1) * tq, tq), tq, False)

    block(pl.multiple_of(qi * tq, tq), tq, True)
    o = (acc_ref[0, :, :DF_DV] / acc_ref[0, :, DF_DV:]
         - lam_ref[0] * (acc_ref[1, :, :DF_DV] / acc_ref[1, :, DF_DV:]))
    o = o * lax.rsqrt(jnp.mean(o * o, axis=-1, keepdims=True) + SUBLN_EPS)
    o_ref[...] = o * sw_ref[...] * (1.0 - lam_init)


def _attn_prompt(q, k, v, b, tp, lam, subln_w, lam_init):
    tq = ATT_TILE
    nq = tp // tq
    return pl.pallas_call(
        functools.partial(_attn_kernel, lam_init),
        grid=(b, DF_HEADS, nq),
        in_specs=[pl.BlockSpec(memory_space=pltpu.SMEM),
                  pl.BlockSpec((tq, DF_DV), lambda bi, h, i: (bi * nq + i, h)),
                  pl.BlockSpec((tp, DF_DV), lambda bi, h, i: (bi, h)),
                  pl.BlockSpec((tp, DF_DV), lambda bi, h, i: (bi, h)),
                  pl.BlockSpec((1, DF_DV), lambda bi, h, i: (0, h))],
        out_specs=pl.BlockSpec((tq, DF_DV), lambda bi, h, i: (bi * nq + i, h)),
        out_shape=jax.ShapeDtypeStruct((b * tp, DF_W), F32),
        scratch_shapes=[pltpu.VMEM((2 * DF_DK, tp), BF16), pltpu.VMEM((tp, 2 * DF_DV), BF16),
                        pltpu.VMEM((2, tq, 1), F32), pltpu.VMEM((2, tq, 2 * DF_DV), F32)],
        compiler_params=_cparams(("parallel", "parallel", "arbitrary")),
        name="diff_attn_prompt",
    )(lam, q, k, v, subln_w)


def _decode_kernel(lam_init, n_pg, pt_ref, lam_ref, q_ref, kn_ref, vn_ref, *refs):
    ck_refs = refs[:n_pg]
    cv_refs = refs[n_pg:2 * n_pg]
    sw_ref, o_ref, qrow_ref, m_ref, l_ref, acc_ref = refs[2 * n_pg:]
    pi = pl.program_id(1)
    n_pages = pl.num_programs(1)
    nq = 2 * DF_HEADS
    page = ck_refs[0].shape[2]

    @pl.when(pi == 0)
    def _():
        rowi = lax.broadcasted_iota(jnp.int32, (nq, DF_W), 0)
        coli = lax.broadcasted_iota(jnp.int32, (nq, DF_W), 1)
        qb = jnp.broadcast_to(q_ref[0] * (DF_DK ** -0.5), (nq, DF_W))
        qrow_ref[...] = jnp.where(jnp.right_shift(coli, int(math.log2(DF_DK))) == rowi, qb, 0.0)
        m_ref[...] = jnp.full(m_ref.shape, -jnp.inf, F32)
        l_ref[...] = jnp.zeros(l_ref.shape, F32)
        acc_ref[...] = jnp.zeros(acc_ref.shape, F32)

    qrow = qrow_ref[...]
    qb = qrow.astype(BF16)
    s = jnp.concatenate([_dot(qb, ck_refs[g][0].astype(BF16)) for g in range(n_pg)], axis=1)
    m_old = m_ref[...]
    m_new = jnp.maximum(m_old, jnp.max(s, axis=-1, keepdims=True))
    alpha = jnp.exp(m_old - m_new)
    pr = jnp.exp(s - m_new)
    l_ref[...] = alpha * l_ref[...] + jnp.sum(pr, axis=-1, keepdims=True)
    prb = pr.astype(BF16)
    for h in range(DF_HEADS):
        pv = None
        for g in range(n_pg):
            v_gh = cv_refs[g][0, pl.ds(h, page, stride=DF_HEADS), :].astype(BF16)
            d = _dot(prb[:, g * page:(g + 1) * page], v_gh)
            pv = d if pv is None else pv + d
        sl = slice(h * DF_DV, (h + 1) * DF_DV)
        acc_ref[:, sl] = alpha * acc_ref[:, sl] + pv
    m_ref[...] = m_new

    @pl.when(pi == n_pages - 1)
    def _():
        s_self = jnp.sum(qrow * kn_ref[0], axis=-1, keepdims=True)
        m_old = m_ref[...]
        m_fin = jnp.maximum(m_old, s_self)
        alpha = jnp.exp(m_old - m_fin)
        p_self = jnp.exp(s_self - m_fin)
        l_fin = alpha * l_ref[...] + p_self
        acc = (alpha * acc_ref[...] + p_self * vn_ref[0]) / l_fin
        for h in range(DF_HEADS):
            sl = slice(h * DF_DV, (h + 1) * DF_DV)
            o = acc[2 * h:2 * h + 1, sl] - lam_ref[0] * acc[2 * h + 1:2 * h + 2, sl]
            o = o * lax.rsqrt(jnp.mean(o * o, axis=-1, keepdims=True) + SUBLN_EPS)
            o_ref[0, :, sl] = o * sw_ref[:, sl] * (1.0 - lam_init)


def _attn_sample(q, k_new, v_new, cache_k, cache_v, page_table, lam, subln_w, lam_init):
    bs = q.shape[0]
    n_pages = page_table.shape[1]
    n_pool, page = cache_k.shape[:2]
    ck = jnp.transpose(cache_k, (0, 2, 3, 4, 1)).reshape(n_pool, DF_W, page)
    cv = cache_v.reshape(n_pool, page * DF_HEADS, DF_DV)
    n_pg = math.gcd(DEC_PAGES, n_pages)
    row = pl.BlockSpec((1, 1, DF_W), lambda b, p, pt: (b, 0, 0))
    ck_specs = [pl.BlockSpec((1, DF_W, page), lambda b, p, pt, g=g: (pt[b, p * n_pg + g], 0, 0))
                for g in range(n_pg)]
    cv_specs = [pl.BlockSpec((1, page * DF_HEADS, DF_DV), lambda b, p, pt, g=g: (pt[b, p * n_pg + g], 0, 0))
                for g in range(n_pg)]
    nq = 2 * DF_HEADS
    grid_spec = pltpu.PrefetchScalarGridSpec(
        num_scalar_prefetch=1,
        grid=(bs, n_pages // n_pg),
        in_specs=[pl.BlockSpec(memory_space=pltpu.SMEM), row, row, row] + ck_specs + cv_specs
                 + [pl.BlockSpec((1, DF_W), lambda b, p, pt: (0, 0))],
        out_specs=row,
        scratch_shapes=[pltpu.VMEM((nq, DF_W), F32), pltpu.VMEM((nq, 1), F32),
                        pltpu.VMEM((nq, 1), F32), pltpu.VMEM((nq, DF_W), F32)],
    )
    out = pl.pallas_call(
        functools.partial(_decode_kernel, lam_init, n_pg),
        grid_spec=grid_spec,
        out_shape=jax.ShapeDtypeStruct((bs, 1, DF_W), F32),
        compiler_params=_cparams(("parallel", "arbitrary")),
        name="diff_attn_paged",
    )(page_table, lam, q.reshape(bs, 1, DF_W), k_new.reshape(bs, 1, DF_W), v_new.reshape(bs, 1, DF_W),
      *([ck] * n_pg), *([cv] * n_pg), subln_w)
    return out.reshape(bs, DF_W)


def _layer_norm(x, w, b):
    mu = jnp.mean(x, axis=-1, keepdims=True)
    xc = x - mu
    var = jnp.mean(xc * xc, axis=-1, keepdims=True)
    return xc * lax.rsqrt(var + LN_EPS) * w + b


def _outproj_kernel(n_main, x_ref, rw_ref, o_ref, rwt_ref, ot_ref, w_ref, lnw_ref, lnb_ref, y_ref, yt_ref):
    main = pl.program_id(0) < n_main
    rw = jnp.where(main, rw_ref[...], rwt_ref[...]).astype(BF16)
    o = jnp.where(main, o_ref[...], ot_ref[...]).astype(BF16)
    mix = _dot(rw, w_ref[:RW_W, :]) + _dot(o, w_ref[RW_W:, :])
    y = _layer_norm(DEEP_ALPHA * x_ref[...] + mix, lnw_ref[...], lnb_ref[...])
    y_ref[...] = y
    yt_ref[...] = y.T.astype(BF16)


def _outproj(x, rw_main, o_main, rw_tail, o_tail, w_bf, ln_w, ln_b):
    rows, d = x.shape
    tm = ROW_TILE
    n_main = rw_main.shape[0] // tm
    assert rw_main.shape[0] % tm == 0 and rows == (n_main + 1) * tm and rw_tail.shape[0] == tm
    row = lambda i: (i, 0)
    main = lambda i: (jnp.minimum(i, n_main - 1), 0)
    tail = lambda i: (0, 0)
    return pl.pallas_call(
        functools.partial(_outproj_kernel, n_main),
        grid=(rows // tm,),
        in_specs=[pl.BlockSpec((tm, d), row), pl.BlockSpec((tm, RW_W), main), pl.BlockSpec((tm, DF_W), main),
                  pl.BlockSpec((tm, RW_W), tail), pl.BlockSpec((tm, DF_W), tail),
                  pl.BlockSpec(w_bf.shape, lambda i: (0, 0)),
                  pl.BlockSpec((1, d), lambda i: (0, 0)), pl.BlockSpec((1, d), lambda i: (0, 0))],
        out_specs=[pl.BlockSpec((tm, d), row), pl.BlockSpec((d, tm), lambda i: (0, i))],
        out_shape=[jax.ShapeDtypeStruct((rows, d), F32), jax.ShapeDtypeStruct((d, rows), BF16)],
        compiler_params=_cparams(("parallel",)),
        name="outproj_ln",
    )(x, rw_main, o_main, rw_tail, o_tail, w_bf, ln_w, ln_b)


def _peer_pairs():
    n = PK_TOPK + 1
    return [(i, j) for i in range(n) for j in range(n) if (i + 1) * (j + 1) <= n]


def _extract_top(cur, n, n_pad):
    vals = []
    rowi = lax.broadcasted_iota(jnp.int32, (n_pad, cur.shape[1]), 0)
    stacked = jnp.zeros((n_pad, cur.shape[1]), F32)
    for i in range(n):
        m = jnp.max(cur, axis=0, keepdims=True)
        vals.append(m)
        stacked = jnp.where(rowi == i, m, stacked)
        cur = jnp.where(cur == m, -jnp.inf, cur)
    return vals, stacked


def _peer_score_kernel(x_ref, wq_ref, sk_ref, sela_ref, selb_ref,
                       e1_ref, e2_ref, psi_ref, qt_ref):
    n_top = PK_TOPK + 1
    half = PK_DQ // 2
    tt = x_ref.shape[0]
    qt_ref[...] = _dot3(wq_ref[...], x_ref[...], ((1,), (1,)))
    n_cand = sela_ref.shape[0]
    n_pad = sela_ref.shape[1]
    cand_row = lax.broadcasted_iota(jnp.int32, (n_cand, tt), 0)
    n_pairs = len(_peer_pairs())
    for h in range(PK_HEADS):
        s1 = _dot3(sk_ref[h, 0], qt_ref[pl.ds(h * PK_DQ, half), :])
        s2 = _dot3(sk_ref[h, 1], qt_ref[pl.ds(h * PK_DQ + half, half), :])
        top1, a_st = _extract_top(s1, n_top, n_pad)
        top2, b_st = _extract_top(s2, n_top, n_pad)
        cand = _dot(sela_ref[...], a_st, precision=HI) + _dot(selb_ref[...], b_st, precision=HI)
        cand = jnp.where(cand_row < n_pairs, cand, -jnp.inf)
        ctop, _ = _extract_top(cand, n_top, n_pad)
        cmax = ctop[0]
        theta = 0.5 * (ctop[PK_TOPK - 1] + ctop[PK_TOPK])
        z = jnp.sum(jnp.where(cand >= theta, jnp.exp(cand - cmax), 0.0), axis=0, keepdims=True)
        e1_ref[h] = jnp.exp(s1 - top1[0]) * (0.5 / z)
        e2_ref[h] = jnp.exp(s2 - top2[0])
        psi_ref[h] = jnp.exp(theta - s1 - top2[0])


def _peer_scores(x1, wq_t, subkeys, sel_a, sel_b):
    rows, d = x1.shape
    tt = PEER_TOK
    out = jax.ShapeDtypeStruct((PK_HEADS, N_KEYS, rows), F32)
    ospec = pl.BlockSpec((PK_HEADS, N_KEYS, tt), lambda i: (0, 0, i))
    return pl.pallas_call(
        _peer_score_kernel,
        grid=(rows // tt,),
        in_specs=[pl.BlockSpec((tt, d), lambda i: (i, 0)),
                  pl.BlockSpec(wq_t.shape, lambda i: (0, 0)),
                  pl.BlockSpec(subkeys.shape, lambda i: (0, 0, 0, 0)),
                  pl.BlockSpec(sel_a.shape, lambda i: (0, 0)),
                  pl.BlockSpec(sel_b.shape, lambda i: (0, 0))],
        out_specs=[ospec] * 3,
        out_shape=[out] * 3,
        scratch_shapes=[pltpu.VMEM((PK_HEADS * PK_DQ, tt), F32)],
        compiler_params=_cparams(("parallel",)),
        name="peer_scores",
    )(x1, wq_t, subkeys, sel_a, sel_b)


def _peer_expert_kernel(x_ref, xt_ref, u_ref, vt_ref, e1_ref, e2_ref, psi_ref, lnw_ref, lnb_ref,
                        y_ref, wt_ref, acc_ref, act_ref, pb_ref, eb_ref):
    ei = pl.program_id(1)
    te = u_ref.shape[0]
    tt = x_ref.shape[0]

    @pl.when(ei == 0)
    def _():
        acc_ref[...] = jnp.zeros(acc_ref.shape, F32)

    n_rows = te // N_KEYS
    k_rows = PEER_KSPLIT // N_KEYS
    n_sub = N_KEYS // SUBLANES
    for j in range(n_rows):
        i1 = ei * n_rows + j
        for h in range(PK_HEADS):
            pb_ref[j, h] = jnp.broadcast_to(psi_ref[h, pl.ds(i1, 1), :], (SUBLANES, tt))
            eb_ref[j, h] = jnp.broadcast_to(e1_ref[h, pl.ds(i1, 1), :], (SUBLANES, tt))
    n_grp = n_rows // k_rows

    def first_product(g):
        act_ref[g % 2] = _dot(u_ref[g * PEER_KSPLIT:(g + 1) * PEER_KSPLIT, :], xt_ref[...])

    def second_product(g):
        ks = slice(g * PEER_KSPLIT, (g + 1) * PEER_KSPLIT)
        return _dot(vt_ref[:, ks], wt_ref[ks, :])

    def gate_row(j):
        rows = slice(j * N_KEYS, (j + 1) * N_KEYS)
        arow = slice((j % k_rows) * N_KEYS, (j % k_rows + 1) * N_KEYS)
        for tg in range(tt // LANES):
            ls = slice(tg * LANES, (tg + 1) * LANES)
            gate = jnp.zeros((n_sub, SUBLANES, LANES), F32)
            for h in range(PK_HEADS):
                e2 = e2_ref[h, :, ls].reshape(n_sub, SUBLANES, LANES)
                gate = gate + jnp.where(e2 >= pb_ref[j, h, :, ls][None], eb_ref[j, h, :, ls][None] * e2, 0.0)
            a = act_ref[(j // k_rows) % 2, arow, ls]
            gate = gate.reshape(N_KEYS, LANES)
            wt_ref[rows, ls] = (gate * (a + a * lax.erf(a * (2.0 ** -0.5)))).astype(BF16)

    first_product(0)
    part = None
    for g in range(n_grp):
        if g + 1 < n_grp:
            first_product(g + 1)
        for r in range(k_rows):
            gate_row(g * k_rows + r)
            if r == 0 and g > 0:
                d = second_product(g - 1)
                part = d if part is None else part + d
    d = second_product(n_grp - 1)
    acc_ref[...] += d if part is None else part + d

    @pl.when(ei == pl.num_programs(1) - 1)
    def _():
        ch = acc_ref[...].T
        y_ref[...] = _layer_norm(DEEP_ALPHA * x_ref[...] + ch, lnw_ref[...], lnb_ref[...])


def _peer_experts(x1, x1_t, u_bf, vt_bf, e1, e2, psi, ln_w, ln_b):
    rows, d = x1.shape
    n_exp = u_bf.shape[0]
    tt = PEER_TOK
    te = PEER_EXP
    tok = lambda i, e: (i, 0)
    sc = pl.BlockSpec((PK_HEADS, N_KEYS, tt), lambda i, e: (0, 0, i))
    vec = pl.BlockSpec((1, d), lambda i, e: (0, 0))
    return pl.pallas_call(
        _peer_expert_kernel,
        grid=(rows // tt, n_exp // te),
        in_specs=[pl.BlockSpec((tt, d), tok), pl.BlockSpec((d, tt), lambda i, e: (0, i)),
                  pl.BlockSpec((te, d), lambda i, e: (e, 0)),
                  pl.BlockSpec((d, te), lambda i, e: (0, e)),
                  sc, sc, sc, vec, vec],
        out_specs=pl.BlockSpec((tt, d), tok),
        out_shape=jax.ShapeDtypeStruct((rows, d), F32),
        scratch_shapes=[pltpu.VMEM((te, tt), BF16), pltpu.VMEM((d, tt), F32),
                        pltpu.VMEM((2, PEER_KSPLIT, tt), F32),
                        pltpu.VMEM((te // N_KEYS, PK_HEADS, SUBLANES, tt), F32),
                        pltpu.VMEM((te // N_KEYS, PK_HEADS, SUBLANES, tt), F32)],
        compiler_params=_cparams(("parallel", "arbitrary")),
        name="peer_experts",
    )(x1, x1_t, u_bf, vt_bf, e1, e2, psi, ln_w, ln_b)


def _rope_tables(pos):
    half = DF_DK // 2
    inv = ROPE_THETA ** (-jnp.arange(half, dtype=F32) / half)
    ang = pos.astype(F32)[:, None] * inv[None, :]
    cos, sin = jnp.cos(ang), jnp.sin(ang)
    cos128 = jnp.concatenate([cos, cos, cos, cos], axis=1)
    sin128 = jnp.concatenate([-sin, sin, -sin, sin], axis=1)
    return cos128, sin128


def _round_up(n, m):
    return -(-n // m) * m


def kernel(x_prompt, x_sample, cache_k, cache_v, page_table, state_wkv, state_shift, meta_tokens, w_in,
           mu_shift, w0, w2_decay, a0, a2, g2, k_k, k_a, r_k, lnx_w, lnx_b, lam_q1, lam_k1, lam_q2, lam_k2,
           subln_w, w_out, ln1_w, ln1_b, pk_wq, pk_subkeys, peer_u, peer_v, ln2_w, ln2_b):
    assert w_in.shape[0] == DEPTH == 1
    bp, seq, d = x_prompt.shape
    bs, dec_seq, _ = x_sample.shape
    assert dec_seq == 1
    t_prompt = N_META + seq
    tp = _round_up(t_prompt, math.lcm(ATT_TILE, PREP_TILE, WKV_CHUNK))
    n_prompt_rows = bp * tp
    rows = _round_up(n_prompt_rows + bs, ROW_TILE)
    page = cache_k.shape[2]
    past = page_table.shape[1] * page
    lam_init = 0.8 - 0.6 * math.exp(-0.3 * 0)
    row2 = lambda a: a.reshape(1, -1)

    meta = jnp.broadcast_to(meta_tokens[None].astype(x_prompt.dtype), (bp, N_META, d))
    xp = jnp.concatenate([meta, x_prompt, jnp.zeros((bp, tp - t_prompt, d), x_prompt.dtype)], axis=1)
    x_all = jnp.concatenate([xp.reshape(n_prompt_rows, d), x_sample.reshape(bs, d),
                             jnp.zeros((rows - n_prompt_rows - bs, d), x_prompt.dtype)], axis=0)
    pos = jnp.concatenate([jnp.tile(jnp.arange(tp), bp), jnp.full((rows - n_prompt_rows,), past)])
    cos, sin = _rope_tables(pos)

    rw_cols, q, k, v = _inproj(x_all, w_in[0].astype(BF16), cos, sin)

    pr = lambda a: a[:n_prompt_rows].reshape(bp, tp, -1)
    sm = lambda a: a[n_prompt_rows:n_prompt_rows + bs]

    hs = jnp.arange(RW_W) // RW_DH
    rw_params = {
        "mu_shift": row2(mu_shift[0]), "w0": row2(w0[0]), "w2_decay": w2_decay[0], "a0": row2(a0[0]),
        "a2": a2[0], "g2": g2[0], "k_k": row2(k_k[0]), "k_a": row2(k_a[0]),
        "headsum": (hs[:, None] == hs[None, :]).astype(F32),
        "r_k": row2(r_k[0]), "lnx_w": row2(lnx_w[0]), "lnx_b": row2(lnx_b[0]),
    }
    prep_p = _rwkv_prep(rw_cols, bp, tp, jnp.zeros((bp, RW_COLS), F32), t_prompt, rw_params)
    rw_out_p, wkv_p = _wkv(prep_p, jnp.zeros((bp, RW_HEADS, RW_DH, RW_DH), F32), rw_params)
    ts = WKV_CHUNK
    rw_s = jnp.concatenate([sm(rw_cols)[:, None, :], jnp.zeros((bs, ts - 1, RW_COLS), F32)], axis=1)
    prep_s = _rwkv_prep(rw_s.reshape(bs * ts, RW_COLS), bs, ts, state_shift[0], dec_seq, rw_params)
    rw_out_s, wkv_s = _wkv(prep_s, state_wkv[0], rw_params)

    f32 = lambda t: t.astype(F32)
    lam = (jnp.exp(jnp.sum(f32(lam_q1[0]) * f32(lam_k1[0])))
           - jnp.exp(jnp.sum(f32(lam_q2[0]) * f32(lam_k2[0]))) + lam_init).reshape(1)
    sw = row2(subln_w[0])
    o_p = _attn_prompt(q, k, v, bp, tp, lam, sw, lam_init)
    o_s = _attn_sample(sm(q), sm(k), sm(v), cache_k[0], cache_v[0], page_table, lam, sw, lam_init)

    tail = jnp.zeros((rows - n_prompt_rows - bs, RW_W), F32)
    x1, x1_t = _outproj(x_all, rw_out_p.reshape(n_prompt_rows, RW_W), o_p,
                        jnp.concatenate([rw_out_s[:, 0, :], tail], axis=0), jnp.concatenate([o_s, tail], axis=0),
                        w_out[0].astype(BF16), row2(ln1_w[0]), row2(ln1_b[0]))

    pairs = _peer_pairs()
    n_cand = _round_up(len(pairs), SUBLANES)
    n_pad = _round_up(PK_TOPK + 1, SUBLANES)
    ia = jnp.array([p[0] for p in pairs] + [0] * (n_cand - len(pairs)))
    ib = jnp.array([p[1] for p in pairs] + [0] * (n_cand - len(pairs)))
    sel_a = (ia[:, None] == jnp.arange(n_pad)[None, :]).astype(F32)
    sel_b = (ib[:, None] == jnp.arange(n_pad)[None, :]).astype(F32)
    e1, e2, psi = _peer_scores(x1, pk_wq[0].T, pk_subkeys[0], sel_a, sel_b)
    y = _peer_experts(x1, x1_t, peer_u[0].astype(BF16), peer_v[0].T.astype(BF16),
                      e1, e2, psi, row2(ln2_w[0]), row2(ln2_b[0]))

    y_prompt = pr(y)[:, N_META:t_prompt]
    y_sample = sm(y).reshape(bs, dec_seq, d)
    k_p = pr(k)[:, :t_prompt].reshape(1, bp, t_prompt, DF_HEADS, 2, DF_DK)
    v_p = pr(v)[:, :t_prompt].reshape(1, bp, t_prompt, DF_HEADS, DF_DV)
    shift_p = pr(rw_cols)[:, t_prompt - 1][None]
    k_s = sm(k).reshape(1, bs, dec_seq, DF_HEADS, 2, DF_DK)
    v_s = sm(v).reshape(1, bs, dec_seq, DF_HEADS, DF_DV)
    shift_s = sm(rw_cols)[None]
    return (y_prompt, y_sample, k_p, v_p, wkv_p[None], shift_p, k_s, v_s, wkv_s[None], shift_s)
```

```python
import functools
import math

import jax
import jax.numpy as jnp
from jax import lax
from jax.experimental import pallas as pl
from jax.experimental.pallas import tpu as pltpu

F32 = jnp.float32
BF16 = jnp.bfloat16

N_META = 16
RW_HEADS = 8
RW_DH = 64
RW_W = RW_HEADS * RW_DH
DECAY_LORA = 64
AAA_LORA = 64
GATE_LORA = 128
RW_COLS = 3 * RW_W + DECAY_LORA + AAA_LORA + GATE_LORA
LNX_EPS = 64e-5
DF_HEADS = 4
DF_DK = 64
DF_DV = 2 * DF_DK
DF_W = DF_HEADS * DF_DV
ROPE_THETA = 10000.0
SUBLN_EPS = 1e-5
LN_EPS = 1e-5
PK_HEADS = 8
N_KEYS = 128
PK_DQ = 128
PK_TOPK = 16
DEPTH = 1
DEEP_ALPHA = (2 * DEPTH) ** 0.25

LANES = 128
SUBLANES = 8
VMEM_LIMIT = 56 * 1024 * 1024

ROW_TILE = 512
PREP_TILE = 128
WKV_CHUNK = 64
WKV_BATCH = 4
ATT_TILE = 384
DEC_PAGES = 16
PEER_TOK = 512
PEER_EXP = 2048
PEER_KSPLIT = 256
HI = lax.Precision.HIGHEST


def _cparams(sem):
    return pltpu.CompilerParams(dimension_semantics=sem, vmem_limit_bytes=VMEM_LIMIT)


def _dot(a, b, dims=((1,), (0,)), precision=None):
    return lax.dot_general(a, b, (dims, ((), ())), precision=precision,
                           preferred_element_type=F32)


def _dot3(a, b, dims=((1,), (0,))):
    a_hi = a.astype(BF16)
    a_lo = (a - a_hi.astype(F32)).astype(BF16)
    b_hi = b.astype(BF16)
    b_lo = (b - b_hi.astype(F32)).astype(BF16)
    return _dot(a_hi, b_hi, dims) + (_dot(a_lo, b_hi, dims) + _dot(a_hi, b_lo, dims))


def _rope128(z, cos, sin_signed, first_half):
    partner = jnp.where(first_half, pltpu.roll(z, LANES - 32, 1), pltpu.roll(z, 32, 1))
    return z * cos + partner * sin_signed


def _inproj_kernel(x_ref, w_ref, cos_ref, sin_ref, rw_ref, q_ref, k_ref, v_ref):
    x = x_ref[...].astype(BF16)
    rw_ref[...] = _dot(x, w_ref[:, :RW_COLS])
    cos = cos_ref[...]
    sin = sin_ref[...]
    lane = lax.broadcasted_iota(jnp.int32, cos.shape, 1)
    first_half = (lane & 32) == 0
    q = _dot(x, w_ref[:, RW_COLS:RW_COLS + DF_W])
    k = _dot(x, w_ref[:, RW_COLS + DF_W:RW_COLS + 2 * DF_W])
    for g in range(DF_W // LANES):
        sl = slice(g * LANES, (g + 1) * LANES)
        q_ref[:, sl] = _rope128(q[:, sl], cos, sin, first_half)
        k_ref[:, sl] = _rope128(k[:, sl], cos, sin, first_half)
    v_ref[...] = _dot(x, w_ref[:, RW_COLS + 2 * DF_W:])


def _inproj(x, w_bf, cos, sin):
    rows, d = x.shape
    in_cols = w_bf.shape[1]
    tm = ROW_TILE
    row = lambda i: (i, 0)
    return pl.pallas_call(
        _inproj_kernel,
        grid=(rows // tm,),
        in_specs=[pl.BlockSpec((tm, d), row),
                  pl.BlockSpec((d, in_cols), lambda i: (0, 0)),
                  pl.BlockSpec((tm, LANES), row),
                  pl.BlockSpec((tm, LANES), row)],
        out_specs=[pl.BlockSpec((tm, RW_COLS), row),
                   pl.BlockSpec((tm, DF_W), row),
                   pl.BlockSpec((tm, DF_W), row),
                   pl.BlockSpec((tm, DF_W), row)],
        out_shape=[jax.ShapeDtypeStruct((rows, RW_COLS), F32),
                   jax.ShapeDtypeStruct((rows, DF_W), F32),
                   jax.ShapeDtypeStruct((rows, DF_W), F32),
                   jax.ShapeDtypeStruct((rows, DF_W), F32)],
        compiler_params=_cparams(("parallel",)),
        name="inproj_rope",
    )(x, w_bf, cos, sin)


def _prep_kernel(t_real, cols_ref, shift0_ref, mu_ref, w0_ref, w2_ref, a0_ref, a2_ref, g2_ref,
                 kk_ref, ka_ref, headsum_ref,
                 r_ref, lw_ref, k_ref, v_ref, kkn_ref, b_ref, g_ref, carry_ref):
    ti = pl.program_id(1)
    tm = cols_ref.shape[0]

    @pl.when(ti == 0)
    def _():
        carry_ref[...] = jnp.broadcast_to(shift0_ref[0], carry_ref.shape)

    x = cols_ref[...]
    rowi = lax.broadcasted_iota(jnp.int32, x.shape, 0)
    prev = jnp.where(rowi == 0, carry_ref[0:1, :], pltpu.roll(x, 1, 0))
    carry_ref[...] = jnp.broadcast_to(x[tm - 1:tm, :], carry_ref.shape)
    mixed = x + (prev - x) * mu_ref[...]
    r = mixed[:, 0:RW_W]
    k = mixed[:, RW_W:2 * RW_W]
    v = mixed[:, 2 * RW_W:3 * RW_W]
    o = 3 * RW_W
    xw = mixed[:, o:o + DECAY_LORA]
    xa = mixed[:, o + DECAY_LORA:o + DECAY_LORA + AAA_LORA]
    xg = mixed[:, o + DECAY_LORA + AAA_LORA:]
    d = w0_ref[...] + _dot3(jnp.tanh(xw), w2_ref[...])
    lw = -jax.nn.sigmoid(d) * math.exp(-0.5)
    a = jax.nn.sigmoid(a0_ref[...] + _dot3(xa, a2_ref[...]))
    g = _dot3(jax.nn.sigmoid(xg), g2_ref[...])
    kk = k * kk_ref[...]
    k2 = k * (1.0 + (a - 1.0) * ka_ref[...])
    ssq = _dot3(kk * kk, headsum_ref[...])
    kkn = kk / jnp.maximum(jnp.sqrt(ssq), 1e-12)
    valid = (ti * tm + lax.broadcasted_iota(jnp.int32, (tm, RW_W), 0)) < t_real
    r_ref[0] = r
    lw_ref[0] = jnp.where(valid, lw, 0.0)
    k_ref[0] = jnp.where(valid, k2, 0.0)
    v_ref[0] = v
    kkn_ref[0] = jnp.where(valid, kkn, 0.0)
    b_ref[0] = jnp.where(valid, kkn * a, 0.0)
    g_ref[0] = g


def _rwkv_prep(cols, bx, tx, shift0, t_real, p):
    tm = min(PREP_TILE, tx)
    blk = lambda b, t: (b, t, 0)
    vec = lambda n: pl.BlockSpec((1, n), lambda b, t: (0, 0))
    mat = lambda r, c: pl.BlockSpec((r, c), lambda b, t: (0, 0))
    outs = [jax.ShapeDtypeStruct((bx, tx, RW_W), F32)] * 7
    return pl.pallas_call(
        functools.partial(_prep_kernel, t_real),
        grid=(bx, tx // tm),
        in_specs=[pl.BlockSpec((tm, RW_COLS), lambda b, t: (b * (tx // tm) + t, 0)),
                  pl.BlockSpec((1, 1, RW_COLS), lambda b, t: (b, 0, 0)),
                  vec(RW_COLS), vec(RW_W), mat(DECAY_LORA, RW_W), vec(RW_W), mat(AAA_LORA, RW_W),
                  mat(GATE_LORA, RW_W), vec(RW_W), vec(RW_W), mat(RW_W, RW_W)],
        out_specs=[pl.BlockSpec((1, tm, RW_W), blk)] * 7,
        out_shape=outs,
        scratch_shapes=[pltpu.VMEM((SUBLANES, RW_COLS), F32)],
        compiler_params=_cparams(("parallel", "arbitrary")),
        name="rwkv_prep",
    )(cols, shift0.reshape(bx, 1, RW_COLS), p["mu_shift"], p["w0"], p["w2_decay"], p["a0"], p["a2"],
      p["g2"], p["k_k"], p["k_a"], p["headsum"])


def _wkv_kernel(r_ref, lw_ref, k_ref, v_ref, kkn_ref, b_ref, g_ref, s0_ref, rk_ref, lnw_ref, lnb_ref,
                out_ref, s_ref):
    ci = pl.program_id(1)
    c = r_ref.shape[1]

    @pl.when(ci == 0)
    def _():
        s_ref[...] = s0_ref[...]

    row = lax.broadcasted_iota(jnp.int32, (c, c), 0)
    col = lax.broadcasted_iota(jnp.int32, (c, c), 1)
    tri = (col <= row).astype(F32)
    nb = r_ref.shape[0]
    a_hat, b_hat, k_hat, r_hat, b_end, k_end, gamma, rkk, v = ([] for _ in range(9))
    for bi in range(nb):
        lw = lw_ref[bi]
        cum = _dot3(tri, lw)
        cum_last = cum[c - 1:c, :]
        g_out = jnp.exp(-cum)
        g_end = jnp.exp(cum_last - cum)
        a_hat.append(-kkn_ref[bi] * jnp.exp(cum - lw))
        b_hat.append(b_ref[bi] * g_out)
        k_hat.append(k_ref[bi] * g_out)
        r_hat.append(r_ref[bi] * jnp.exp(cum))
        b_end.append(b_ref[bi] * g_end)
        k_end.append(k_ref[bi] * g_end)
        gamma.append(jnp.exp(cum_last))
        rkk.append(r_ref[bi] * k_ref[bi] * rk_ref[...])
        v.append(v_ref[bi])

    row2 = lax.broadcasted_iota(jnp.int32, (2 * c, 2 * c), 0)
    col2 = lax.broadcasted_iota(jnp.int32, (2 * c, 2 * c), 1)
    rt = jnp.where(row2 >= c, row2 - c, row2)
    cs = jnp.where(col2 >= c, col2 - c, col2)
    keep = cs < rt + jnp.where(row2 >= c, 1, 0)

    eye = (row == col).astype(F32)
    n_sq = int(math.log2(c))
    probs = [(bi, h) for bi in range(nb) for h in range(RW_HEADS)]
    n = range(len(probs))
    hsl = [slice(h * RW_DH, (h + 1) * RW_DH) for _, h in probs]
    bf = lambda x: x.astype(BF16)
    s_old = [s_ref[bi, h] for bi, h in probs]
    a_h = [a_hat[bi][:, hsl[i]] for i, (bi, _) in enumerate(probs)]
    v_h = [v[bi][:, hsl[i]] for i, (bi, _) in enumerate(probs)]
    r_h = [r_hat[bi][:, hsl[i]] for i, (bi, _) in enumerate(probs)]
    vb = [bf(x) for x in v_h]
    prod = [jnp.where(keep, _dot(bf(jnp.concatenate([a_h[i], r_h[i]], axis=0)),
                                 bf(jnp.concatenate([b_hat[bi][:, hsl[i]], k_hat[bi][:, hsl[i]]], axis=0)),
                                 ((1,), (1,))), 0.0) for i, (bi, _) in enumerate(probs)]
    p = [prod[i][:c, :c] for i in n]
    t = [eye + p[i] for i in n]
    w = [_dot(bf(prod[i][:c, c:]), vb[i]) for i in n]
    for _ in range(n_sq - 1):
        pb = [bf(p[i]) for i in n]
        p = [_dot(pb[i], pb[i]) for i in n]
        t = [t[i] + _dot(bf(p[i]), bf(t[i])) for i in n]
    tb = [bf(t[i]) for i in n]
    u0 = [_dot(tb[i], bf(w[i])) for i in n]
    ta = [_dot(tb[i], bf(a_h[i])) for i in n]
    xs = [_dot(bf(jnp.concatenate([ta[i], r_h[i]], axis=0)), bf(s_old[i]), ((1,), (1,))) for i in n]
    uv = [bf(jnp.concatenate([xs[i][:c] + u0[i], v_h[i]], axis=0)) for i in n]
    s_new = [s_old[i] * gamma[bi][:, hsl[i]]
             + _dot(uv[i], bf(jnp.concatenate([b_end[bi][:, hsl[i]], k_end[bi][:, hsl[i]]], axis=0)),
                    ((0,), (0,))) for i, (bi, _) in enumerate(probs)]
    y = [xs[i][c:] + _dot(bf(prod[i][c:, :]), uv[i]) for i in n]
    for i, (bi, h) in enumerate(probs):
        s_ref[bi, h] = s_new[i]
    for i, (bi, h) in enumerate(probs):
        sl = hsl[i]
        mu = jnp.mean(y[i], axis=-1, keepdims=True)
        yc = y[i] - mu
        var = jnp.mean(yc * yc, axis=-1, keepdims=True)
        yn = yc * lax.rsqrt(var + LNX_EPS) * lnw_ref[:, sl] + lnb_ref[:, sl]
        bonus = jnp.sum(rkk[bi][:, sl], axis=-1, keepdims=True) * v_h[i]
        out_ref[bi, :, sl] = (yn + bonus) * g_ref[bi, :, sl]


def _wkv(prep, s0, p):
    r = prep[0]
    bx, tx, _ = r.shape
    c = WKV_CHUNK
    nb = WKV_BATCH
    blk = pl.BlockSpec((nb, c, RW_W), lambda b, t: (b, t, 0))
    vec = pl.BlockSpec((1, RW_W), lambda b, t: (0, 0))
    st = pl.BlockSpec((nb, RW_HEADS, RW_DH, RW_DH), lambda b, t: (b, 0, 0, 0))
    return pl.pallas_call(
        _wkv_kernel,
        grid=(bx // nb, tx // c),
        in_specs=[blk] * 7 + [st, vec, vec, vec],
        out_specs=[blk, st],
        out_shape=[jax.ShapeDtypeStruct((bx, tx, RW_W), F32),
                   jax.ShapeDtypeStruct((bx, RW_HEADS, RW_DH, RW_DH), F32)],
        compiler_params=_cparams(("parallel", "arbitrary")),
        name="wkv_chunk",
    )(*prep, s0, p["r_k"], p["lnx_w"], p["lnx_b"])


def _attn_kernel(lam_init, lam_ref, q_ref, k_ref, v_ref, sw_ref, o_ref, kt_ref, vx_ref, m_ref, acc_ref):
    qi = pl.program_id(2)
    tq = q_ref.shape[0]

    @pl.when(qi == 0)
    def _():
        kt_ref[...] = k_ref[...].T.astype(BF16)
        vx_ref[:, :DF_DV] = v_ref[...].astype(BF16)
        vx_ref[:, DF_DV:] = jnp.ones((vx_ref.shape[0], DF_DV), BF16)

    q = (q_ref[...] * (DF_DK ** -0.5)).astype(BF16)
    qc = [q[:, :DF_DK], q[:, DF_DK:]]
    m_ref[...] = jnp.full(m_ref.shape, -jnp.inf, F32)
    acc_ref[...] = jnp.zeros(acc_ref.shape, F32)
    rowi = lax.broadcasted_iota(jnp.int32, (tq, tq), 0)
    coli = lax.broadcasted_iota(jnp.int32, (tq, tq), 1)
    causal = coli <= rowi

    def block(start, width, masked):
        vblk = vx_ref[pl.ds(start, width), :]
        s = [_dot(qc[c], kt_ref[c * DF_DK:(c + 1) * DF_DK, pl.ds(start, width)]) for c in range(2)]
        if masked:
            s = [jnp.where(causal, s[c], -jnp.inf) for c in range(2)]
        m_old = [m_ref[c] for c in range(2)]
        m_new = [jnp.maximum(m_old[c], jnp.max(s[c], axis=-1, keepdims=True)) for c in range(2)]
        pr = [jnp.exp(s[c] - m_new[c]).astype(BF16) for c in range(2)]
        pv = [_dot(pr[c], vblk) for c in range(2)]
        for c in range(2):
            acc_ref[c] = jnp.exp(m_old[c] - m_new[c]) * acc_ref[c] + pv[c]
            m_ref[c] = m_new[c]

    def pair(pj, carry):
        block(pl.multiple_of(pj * (2 * tq), 2 * tq), 2 * tq, False)
        return carry

    lax.fori_loop(0, qi // 2, pair, 0)

    @pl.when(qi % 2 == 1)
    def _():
        block(pl.multiple_of((qi - 1) * tq, tq), tq, False)

    block(pl.multiple_of(qi * tq, tq), tq, True)
    o = (acc_ref[0, :, :DF_DV] / acc_ref[0, :, DF_DV:]
         - lam_ref[0] * (acc_ref[1, :, :DF_DV] / acc_ref[1, :, DF_DV:]))
    o = o * lax.rsqrt(jnp.mean(o * o, axis=-1, keepdims=True) + SUBLN_EPS)
    o_ref[...] = o * sw_ref[...] * (1.0 - lam_init)


def _attn_prompt(q, k, v, b, tp, lam, subln_w, lam_init):
    tq = ATT_TILE
    nq = tp // tq
    return pl.pallas_call(
        functools.partial(_attn_kernel, lam_init),
        grid=(b, DF_HEADS, nq),
        in_specs=[pl.BlockSpec(memory_space=pltpu.SMEM),
                  pl.BlockSpec((tq, DF_DV), lambda bi, h, i: (bi * nq + i, h)),
                  pl.BlockSpec((tp, DF_DV), lambda bi, h, i: (bi, h)),
                  pl.BlockSpec((tp, DF_DV), lambda bi, h, i: (bi, h)),
                  pl.BlockSpec((1, DF_DV), lambda bi, h, i: (0, h))],
        out_specs=pl.BlockSpec((tq, DF_DV), lambda bi, h, i: (bi * nq + i, h)),
        out_shape=jax.ShapeDtypeStruct((b * tp, DF_W), F32),
        scratch_shapes=[pltpu.VMEM((2 * DF_DK, tp), BF16), pltpu.VMEM((tp, 2 * DF_DV), BF16),
                        pltpu.VMEM((2, tq, 1), F32), pltpu.VMEM((2, tq, 2 * DF_DV), F32)],
        compiler_params=_cparams(("parallel", "parallel", "arbitrary")),
        name="diff_attn_prompt",
    )(lam, q, k, v, subln_w)


def _decode_kernel(lam_init, n_pg, pt_ref, lam_ref, q_ref, kn_ref, vn_ref, *refs):
    ck_refs = refs[:n_pg]
    cv_refs = refs[n_pg:2 * n_pg]
    sw_ref, o_ref, qrow_ref, m_ref, l_ref, acc_ref = refs[2 * n_pg:]
    pi = pl.program_id(1)
    n_pages = pl.num_programs(1)
    nq = 2 * DF_HEADS
    page = ck_refs[0].shape[2]

    @pl.when(pi == 0)
    def _():
        rowi = lax.broadcasted_iota(jnp.int32, (nq, DF_W), 0)
        coli = lax.broadcasted_iota(jnp.int32, (nq, DF_W), 1)
        qb = jnp.broadcast_to(q_ref[0] * (DF_DK ** -0.5), (nq, DF_W))
        qrow_ref[...] = jnp.where(jnp.right_shift(coli, int(math.log2(DF_DK))) == rowi, qb, 0.0)
        m_ref[...] = jnp.full(m_ref.shape, -jnp.inf, F32)
        l_ref[...] = jnp.zeros(l_ref.shape, F32)
        acc_ref[...] = jnp.zeros(acc_ref.shape, F32)

    qrow = qrow_ref[...]
    qb = qrow.astype(BF16)
    s = jnp.concatenate([_dot(qb, ck_refs[g][0].astype(BF16)) for g in range(n_pg)], axis=1)
    m_old = m_ref[...]
    m_new = jnp.maximum(m_old, jnp.max(s, axis=-1, keepdims=True))
    alpha = jnp.exp(m_old - m_new)
    pr = jnp.exp(s - m_new)
    l_ref[...] = alpha * l_ref[...] + jnp.sum(pr, axis=-1, keepdims=True)
    prb = pr.astype(BF16)
    for h in range(DF_HEADS):
        pv = None
        for g in range(n_pg):
            v_gh = cv_refs[g][0, pl.ds(h, page, stride=DF_HEADS), :].astype(BF16)
            d = _dot(prb[:, g * page:(g + 1) * page], v_gh)
            pv = d if pv is None else pv + d
        sl = slice(h * DF_DV, (h + 1) * DF_DV)
        acc_ref[:, sl] = alpha * acc_ref[:, sl] + pv
    m_ref[...] = m_new

    @pl.when(pi == n_pages - 1)
    def _():
        s_self = jnp.sum(qrow * kn_ref[0], axis=-1, keepdims=True)
        m_old = m_ref[...]
        m_fin = jnp.maximum(m_old, s_self)
        alpha = jnp.exp(m_old - m_fin)
        p_self = jnp.exp(s_self - m_fin)
        l_fin = alpha * l_ref[...] + p_self
        acc = (alpha * acc_ref[...] + p_self * vn_ref[0]) / l_fin
        for h in range(DF_HEADS):
            sl = slice(h * DF_DV, (h + 1) * DF_DV)
            o = acc[2 * h:2 * h + 1, sl] - lam_ref[0] * acc[2 * h + 1:2 * h + 2, sl]
            o = o * lax.rsqrt(jnp.mean(o * o, axis=-1, keepdims=True) + SUBLN_EPS)
            o_ref[0, :, sl] = o * sw_ref[:, sl] * (1.0 - lam_init)


def _attn_sample(q, k_new, v_new, cache_k, cache_v, page_table, lam, subln_w, lam_init):
    bs = q.shape[0]
    n_pages = page_table.shape[1]
    n_pool, page = cache_k.shape[:2]
    ck = jnp.transpose(cache_k, (0, 2, 3, 4, 1)).reshape(n_pool, DF_W, page)
    cv = cache_v.reshape(n_pool, page * DF_HEADS, DF_DV)
    n_pg = math.gcd(DEC_PAGES, n_pages)
    row = pl.BlockSpec((1, 1, DF_W), lambda b, p, pt: (b, 0, 0))
    ck_specs = [pl.BlockSpec((1, DF_W, page), lambda b, p, pt, g=g: (pt[b, p * n_pg + g], 0, 0))
                for g in range(n_pg)]
    cv_specs = [pl.BlockSpec((1, page * DF_HEADS, DF_DV), lambda b, p, pt, g=g: (pt[b, p * n_pg + g], 0, 0))
                for g in range(n_pg)]
    nq = 2 * DF_HEADS
    grid_spec = pltpu.PrefetchScalarGridSpec(
        num_scalar_prefetch=1,
        grid=(bs, n_pages // n_pg),
        in_specs=[pl.BlockSpec(memory_space=pltpu.SMEM), row, row, row] + ck_specs + cv_specs
                 + [pl.BlockSpec((1, DF_W), lambda b, p, pt: (0, 0))],
        out_specs=row,
        scratch_shapes=[pltpu.VMEM((nq, DF_W), F32), pltpu.VMEM((nq, 1), F32),
                        pltpu.VMEM((nq, 1), F32), pltpu.VMEM((nq, DF_W), F32)],
    )
    out = pl.pallas_call(
        functools.partial(_decode_kernel, lam_init, n_pg),
        grid_spec=grid_spec,
        out_shape=jax.ShapeDtypeStruct((bs, 1, DF_W), F32),
        compiler_params=_cparams(("parallel", "arbitrary")),
        name="diff_attn_paged",
    )(page_table, lam, q.reshape(bs, 1, DF_W), k_new.reshape(bs, 1, DF_W), v_new.reshape(bs, 1, DF_W),
      *([ck] * n_pg), *([cv] * n_pg), subln_w)
    return out.reshape(bs, DF_W)


def _layer_norm(x, w, b):
    mu = jnp.mean(x, axis=-1, keepdims=True)
    xc = x - mu
    var = jnp.mean(xc * xc, axis=-1, keepdims=True)
    return xc * lax.rsqrt(var + LN_EPS) * w + b


def _outproj_kernel(n_main, x_ref, rw_ref, o_ref, rwt_ref, ot_ref, w_ref, lnw_ref, lnb_ref, y_ref, yt_ref):
    main = pl.program_id(0) < n_main
    rw = jnp.where(main, rw_ref[...], rwt_ref[...]).astype(BF16)
    o = jnp.where(main, o_ref[...], ot_ref[...]).astype(BF16)
    mix = _dot(rw, w_ref[:RW_W, :]) + _dot(o, w_ref[RW_W:, :])
    y = _layer_norm(DEEP_ALPHA * x_ref[...] + mix, lnw_ref[...], lnb_ref[...])
    y_ref[...] = y
    yt_ref[...] = y.T.astype(BF16)


def _outproj(x, rw_main, o_main, rw_tail, o_tail, w_bf, ln_w, ln_b):
    rows, d = x.shape
    tm = ROW_TILE
    n_main = rw_main.shape[0] // tm
    assert rw_main.shape[0] % tm == 0 and rows == (n_main + 1) * tm and rw_tail.shape[0] == tm
    row = lambda i: (i, 0)
    main = lambda i: (jnp.minimum(i, n_main - 1), 0)
    tail = lambda i: (0, 0)
    return pl.pallas_call(
        functools.partial(_outproj_kernel, n_main),
        grid=(rows // tm,),
        in_specs=[pl.BlockSpec((tm, d), row), pl.BlockSpec((tm, RW_W), main), pl.BlockSpec((tm, DF_W), main),
                  pl.BlockSpec((tm, RW_W), tail), pl.BlockSpec((tm, DF_W), tail),
                  pl.BlockSpec(w_bf.shape, lambda i: (0, 0)),
                  pl.BlockSpec((1, d), lambda i: (0, 0)), pl.BlockSpec((1, d), lambda i: (0, 0))],
        out_specs=[pl.BlockSpec((tm, d), row), pl.BlockSpec((d, tm), lambda i: (0, i))],
        out_shape=[jax.ShapeDtypeStruct((rows, d), F32), jax.ShapeDtypeStruct((d, rows), BF16)],
        compiler_params=_cparams(("parallel",)),
        name="outproj_ln",
    )(x, rw_main, o_main, rw_tail, o_tail, w_bf, ln_w, ln_b)


def _peer_pairs():
    n = PK_TOPK + 1
    return [(i, j) for i in range(n) for j in range(n) if (i + 1) * (j + 1) <= n]


def _sort_pairs(n):
    pairs = []
    p = 1
    while p < n:
        k = p
        while k >= 1:
            for j in range(k % p, n - k, 2 * k):
                for i in range(min(k, n - j - k)):
                    if (i + j) // (2 * p) == (i + j + k) // (2 * p):
                        pairs.append((i + j, i + j + k))
            k //= 2
        p *= 2
    return pairs


def _extract_top(cur, n, n_pad):
    rows, cols = cur.shape
    n_tiles = rows // SUBLANES
    size = 1 << (n_tiles - 1).bit_length()
    neg = jnp.full((SUBLANES, cols), -jnp.inf, F32)
    t = [cur[i * SUBLANES:(i + 1) * SUBLANES, :] for i in range(n_tiles)] + [neg] * (size - n_tiles)
    for i, j in _sort_pairs(size):
        t[i], t[j] = jnp.maximum(t[i], t[j]), jnp.minimum(t[i], t[j])
    vals = []
    rowi = lax.broadcasted_iota(jnp.int32, (n_pad, cols), 0)
    stacked = jnp.zeros((n_pad, cols), F32)
    for k in range(n):
        m = jnp.max(t[0], axis=0, keepdims=True)
        vals.append(m)
        stacked = jnp.where(rowi == k, m, stacked)
        hit = t[0] == m
        keep = min(len(t), n - 1 - k)
        t = [jnp.where(hit, t[l + 1] if l + 1 < len(t) else neg, t[l]) for l in range(keep)]
    return vals, stacked


def _peer_score_kernel(x_ref, wq_ref, sk_ref, sela_ref, selb_ref,
                       e1_ref, e2_ref, psi_ref, qt_ref):
    n_top = PK_TOPK + 1
    half = PK_DQ // 2
    tt = x_ref.shape[0]
    qt_ref[...] = _dot3(wq_ref[...], x_ref[...], ((1,), (1,)))
    n_cand = sela_ref.shape[0]
    n_pad = sela_ref.shape[1]
    cand_row = lax.broadcasted_iota(jnp.int32, (n_cand, tt), 0)
    n_pairs = len(_peer_pairs())
    for h in range(PK_HEADS):
        s1 = _dot3(sk_ref[h, 0], qt_ref[pl.ds(h * PK_DQ, half), :])
        s2 = _dot3(sk_ref[h, 1], qt_ref[pl.ds(h * PK_DQ + half, half), :])
        top1, a_st = _extract_top(s1, n_top, n_pad)
        top2, b_st = _extract_top(s2, n_top, n_pad)
        cand = _dot(sela_ref[...], a_st, precision=HI) + _dot(selb_ref[...], b_st, precision=HI)
        cand = jnp.where(cand_row < n_pairs, cand, -jnp.inf)
        ctop, _ = _extract_top(cand, n_top, n_pad)
        cmax = ctop[0]
        theta = 0.5 * (ctop[PK_TOPK - 1] + ctop[PK_TOPK])
        z = jnp.sum(jnp.where(cand >= theta, jnp.exp(cand - cmax), 0.0), axis=0, keepdims=True)
        e1_ref[h] = jnp.exp(s1 - top1[0]) * (0.5 / z)
        e2_ref[h] = jnp.exp(s2 - top2[0])
        psi_ref[h] = jnp.exp(theta - s1 - top2[0])


def _peer_scores(x1, wq_t, subkeys, sel_a, sel_b):
    rows, d = x1.shape
    tt = PEER_TOK
    out = jax.ShapeDtypeStruct((PK_HEADS, N_KEYS, rows), F32)
    ospec = pl.BlockSpec((PK_HEADS, N_KEYS, tt), lambda i: (0, 0, i))
    return pl.pallas_call(
        _peer_score_kernel,
        grid=(rows // tt,),
        in_specs=[pl.BlockSpec((tt, d), lambda i: (i, 0)),
                  pl.BlockSpec(wq_t.shape, lambda i: (0, 0)),
                  pl.BlockSpec(subkeys.shape, lambda i: (0, 0, 0, 0)),
                  pl.BlockSpec(sel_a.shape, lambda i: (0, 0)),
                  pl.BlockSpec(sel_b.shape, lambda i: (0, 0))],
        out_specs=[ospec] * 3,
        out_shape=[out] * 3,
        scratch_shapes=[pltpu.VMEM((PK_HEADS * PK_DQ, tt), F32)],
        compiler_params=_cparams(("parallel",)),
        name="peer_scores",
    )(x1, wq_t, subkeys, sel_a, sel_b)


def _peer_expert_kernel(x_ref, xt_ref, u_ref, vt_ref, e1_ref, e2_ref, psi_ref, lnw_ref, lnb_ref,
                        y_ref, wt_ref, acc_ref, act_ref, pb_ref, eb_ref):
    ei = pl.program_id(1)
    te = u_ref.shape[0]
    tt = x_ref.shape[0]

    @pl.when(ei == 0)
    def _():
        acc_ref[...] = jnp.zeros(acc_ref.shape, F32)

    n_rows = te // N_KEYS
    k_rows = PEER_KSPLIT // N_KEYS
    n_sub = N_KEYS // SUBLANES
    for j in range(n_rows):
        i1 = ei * n_rows + j
        for h in range(PK_HEADS):
            pb_ref[j, h] = jnp.broadcast_to(psi_ref[h, pl.ds(i1, 1), :], (SUBLANES, tt))
            eb_ref[j, h] = jnp.broadcast_to(e1_ref[h, pl.ds(i1, 1), :], (SUBLANES, tt))
    n_grp = n_rows // k_rows

    def first_product(g):
        act_ref[g % 2] = _dot(u_ref[g * PEER_KSPLIT:(g + 1) * PEER_KSPLIT, :], xt_ref[...])

    def second_product(g):
        ks = slice(g * PEER_KSPLIT, (g + 1) * PEER_KSPLIT)
        return _dot(vt_ref[:, ks], wt_ref[ks, :])

    def gate_row(j):
        rows = slice(j * N_KEYS, (j + 1) * N_KEYS)
        arow = slice((j % k_rows) * N_KEYS, (j % k_rows + 1) * N_KEYS)
        for tg in range(tt // LANES):
            ls = slice(tg * LANES, (tg + 1) * LANES)
            gate = jnp.zeros((n_sub, SUBLANES, LANES), F32)
            for h in range(PK_HEADS):
                e2 = e2_ref[h, :, ls].reshape(n_sub, SUBLANES, LANES)
                gate = gate + jnp.where(e2 >= pb_ref[j, h, :, ls][None], eb_ref[j, h, :, ls][None] * e2, 0.0)
            a = act_ref[(j // k_rows) % 2, arow, ls]
            gate = gate.reshape(N_KEYS, LANES)
            wt_ref[rows, ls] = (gate * (a + a * lax.erf(a * (2.0 ** -0.5)))).astype(BF16)

    first_product(0)
    part = None
    for g in range(n_grp):
        if g + 1 < n_grp:
            first_product(g + 1)
        for r in range(k_rows):
            gate_row(g * k_rows + r)
            if r == 0 and g > 0:
                d = second_product(g - 1)
                part = d if part is None else part + d
    d = second_product(n_grp - 1)
    acc_ref[...] += d if part is None else part + d

    @pl.when(ei == pl.num_programs(1) - 1)
    def _():
        ch = acc_ref[...].T
        y_ref[...] = _layer_norm(DEEP_ALPHA * x_ref[...] + ch, lnw_ref[...], lnb_ref[...])


def _peer_experts(x1, x1_t, u_bf, vt_bf, e1, e2, psi, ln_w, ln_b):
    rows, d = x1.shape
    n_exp = u_bf.shape[0]
    tt = PEER_TOK
    te = PEER_EXP
    tok = lambda i, e: (i, 0)
    sc = pl.BlockSpec((PK_HEADS, N_KEYS, tt), lambda i, e: (0, 0, i))
    vec = pl.BlockSpec((1, d), lambda i, e: (0, 0))
    return pl.pallas_call(
        _peer_expert_kernel,
        grid=(rows // tt, n_exp // te),
        in_specs=[pl.BlockSpec((tt, d), tok), pl.BlockSpec((d, tt), lambda i, e: (0, i)),
                  pl.BlockSpec((te, d), lambda i, e: (e, 0)),
                  pl.BlockSpec((d, te), lambda i, e: (0, e)),
                  sc, sc, sc, vec, vec],
        out_specs=pl.BlockSpec((tt, d), tok),
        out_shape=jax.ShapeDtypeStruct((rows, d), F32),
        scratch_shapes=[pltpu.VMEM((te, tt), BF16), pltpu.VMEM((d, tt), F32),
                        pltpu.VMEM((2, PEER_KSPLIT, tt), F32),
                        pltpu.VMEM((te // N_KEYS, PK_HEADS, SUBLANES, tt), F32),
                        pltpu.VMEM((te // N_KEYS, PK_HEADS, SUBLANES, tt), F32)],
        compiler_params=_cparams(("parallel", "arbitrary")),
        name="peer_experts",
    )(x1, x1_t, u_bf, vt_bf, e1, e2, psi, ln_w, ln_b)


def _rope_tables(pos):
    half = DF_DK // 2
    inv = ROPE_THETA ** (-jnp.arange(half, dtype=F32) / half)
    ang = pos.astype(F32)[:, None] * inv[None, :]
    cos, sin = jnp.cos(ang), jnp.sin(ang)
    cos128 = jnp.concatenate([cos, cos, cos, cos], axis=1)
    sin128 = jnp.concatenate([-sin, sin, -sin, sin], axis=1)
    return cos128, sin128


def _round_up(n, m):
    return -(-n // m) * m


def kernel(x_prompt, x_sample, cache_k, cache_v, page_table, state_wkv, state_shift, meta_tokens, w_in,
           mu_shift, w0, w2_decay, a0, a2, g2, k_k, k_a, r_k, lnx_w, lnx_b, lam_q1, lam_k1, lam_q2, lam_k2,
           subln_w, w_out, ln1_w, ln1_b, pk_wq, pk_subkeys, peer_u, peer_v, ln2_w, ln2_b):
    assert w_in.shape[0] == DEPTH == 1
    bp, seq, d = x_prompt.shape
    bs, dec_seq, _ = x_sample.shape
    assert dec_seq == 1
    t_prompt = N_META + seq
    tp = _round_up(t_prompt, math.lcm(ATT_TILE, PREP_TILE, WKV_CHUNK))
    n_prompt_rows = bp * tp
    rows = _round_up(n_prompt_rows + bs, ROW_TILE)
    page = cache_k.shape[2]
    past = page_table.shape[1] * page
    lam_init = 0.8 - 0.6 * math.exp(-0.3 * 0)
    row2 = lambda a: a.reshape(1, -1)

    meta = meta_tokens.astype(x_prompt.dtype)
    pad = jnp.zeros((tp - t_prompt, d), x_prompt.dtype)
    pieces = [p for b in range(bp) for p in (meta, x_prompt[b], pad)]
    x_all = jnp.concatenate(pieces + [x_sample.reshape(bs, d),
                                      jnp.zeros((rows - n_prompt_rows - bs, d), x_prompt.dtype)], axis=0)
    pos = jnp.concatenate([jnp.tile(jnp.arange(tp), bp), jnp.full((rows - n_prompt_rows,), past)])
    cos, sin = _rope_tables(pos)

    rw_cols, q, k, v = _inproj(x_all, w_in[0].astype(BF16), cos, sin)

    pr = lambda a: a[:n_prompt_rows].reshape(bp, tp, -1)
    sm = lambda a: a[n_prompt_rows:n_prompt_rows + bs]

    hs = jnp.arange(RW_W) // RW_DH
    rw_params = {
        "mu_shift": row2(mu_shift[0]), "w0": row2(w0[0]), "w2_decay": w2_decay[0], "a0": row2(a0[0]),
        "a2": a2[0], "g2": g2[0], "k_k": row2(k_k[0]), "k_a": row2(k_a[0]),
        "headsum": (hs[:, None] == hs[None, :]).astype(F32),
        "r_k": row2(r_k[0]), "lnx_w": row2(lnx_w[0]), "lnx_b": row2(lnx_b[0]),
    }
    prep_p = _rwkv_prep(rw_cols, bp, tp, jnp.zeros((bp, RW_COLS), F32), t_prompt, rw_params)
    rw_out_p, wkv_p = _wkv(prep_p, jnp.zeros((bp, RW_HEADS, RW_DH, RW_DH), F32), rw_params)
    ts = WKV_CHUNK
    rw_s = jnp.concatenate([sm(rw_cols)[:, None, :], jnp.zeros((bs, ts - 1, RW_COLS), F32)], axis=1)
    prep_s = _rwkv_prep(rw_s.reshape(bs * ts, RW_COLS), bs, ts, state_shift[0], dec_seq, rw_params)
    rw_out_s, wkv_s = _wkv(prep_s, state_wkv[0], rw_params)

    f32 = lambda t: t.astype(F32)
    lam = (jnp.exp(jnp.sum(f32(lam_q1[0]) * f32(lam_k1[0])))
           - jnp.exp(jnp.sum(f32(lam_q2[0]) * f32(lam_k2[0]))) + lam_init).reshape(1)
    sw = row2(subln_w[0])
    o_p = _attn_prompt(q, k, v, bp, tp, lam, sw, lam_init)
    o_s = _attn_sample(sm(q), sm(k), sm(v), cache_k[0], cache_v[0], page_table, lam, sw, lam_init)

    tail = jnp.zeros((rows - n_prompt_rows - bs, RW_W), F32)
    x1, x1_t = _outproj(x_all, rw_out_p.reshape(n_prompt_rows, RW_W), o_p,
                        jnp.concatenate([rw_out_s[:, 0, :], tail], axis=0), jnp.concatenate([o_s, tail], axis=0),
                        w_out[0].astype(BF16), row2(ln1_w[0]), row2(ln1_b[0]))

    pairs = _peer_pairs()
    n_cand = _round_up(len(pairs), SUBLANES)
    n_pad = _round_up(PK_TOPK + 1, SUBLANES)
    ia = jnp.array([p[0] for p in pairs] + [0] * (n_cand - len(pairs)))
    ib = jnp.array([p[1] for p in pairs] + [0] * (n_cand - len(pairs)))
    sel_a = (ia[:, None] == jnp.arange(n_pad)[None, :]).astype(F32)
    sel_b = (ib[:, None] == jnp.arange(n_pad)[None, :]).astype(F32)
    e1, e2, psi = _peer_scores(x1, pk_wq[0].T, pk_subkeys[0], sel_a, sel_b)
    y = _peer_experts(x1, x1_t, peer_u[0].astype(BF16), peer_v[0].T.astype(BF16),
                      e1, e2, psi, row2(ln2_w[0]), row2(ln2_b[0]))

    y_prompt = pr(y)[:, N_META:t_prompt]
    y_sample = sm(y).reshape(bs, dec_seq, d)
    k_p = pr(k)[:, :t_prompt].reshape(1, bp, t_prompt, DF_HEADS, 2, DF_DK)
    v_p = pr(v)[:, :t_prompt].reshape(1, bp, t_prompt, DF_HEADS, DF_DV)
    shift_p = pr(rw_cols)[:, t_prompt - 1][None]
    k_s = sm(k).reshape(1, bs, dec_seq, DF_HEADS, 2, DF_DK)
    v_s = sm(v).reshape(1, bs, dec_seq, DF_HEADS, DF_DV)
    shift_s = sm(rw_cols)[None]
    return (y_prompt, y_sample, k_p, v_p, wkv_p[None], shift_p, k_s, v_s, wkv_s[None], shift_s)
```

```python
import functools
import math

import jax
import jax.numpy as jnp
from jax import lax
from jax.experimental import pallas as pl
from jax.experimental.pallas import tpu as pltpu

F32 = jnp.float32
BF16 = jnp.bfloat16

N_META = 16
RW_HEADS = 8
RW_DH = 64
RW_W = RW_HEADS * RW_DH
DECAY_LORA = 64
AAA_LORA = 64
GATE_LORA = 128
RW_COLS = 3 * RW_W + DECAY_LORA + AAA_LORA + GATE_LORA
LNX_EPS = 64e-5
DF_HEADS = 4
DF_DK = 64
DF_DV = 2 * DF_DK
DF_W = DF_HEADS * DF_DV
ROPE_THETA = 10000.0
SUBLN_EPS = 1e-5
LN_EPS = 1e-5
PK_HEADS = 8
N_KEYS = 128
PK_DQ = 128
PK_TOPK = 16
DEPTH = 1
DEEP_ALPHA = (2 * DEPTH) ** 0.25

LANES = 128
SUBLANES = 8
VMEM_LIMIT = 56 * 1024 * 1024

ROW_TILE = 512
PREP_TILE = 128
WKV_CHUNK = 64
WKV_BATCH = 4
ATT_TILE = 384
DEC_PAGES = 16
PEER_TOK = 512
PEER_EXP = 2048
PEER_KSPLIT = 256
HI = lax.Precision.HIGHEST


def _cparams(sem):
    return pltpu.CompilerParams(dimension_semantics=sem, vmem_limit_bytes=VMEM_LIMIT)


def _dot(a, b, dims=((1,), (0,)), precision=None):
    return lax.dot_general(a, b, (dims, ((), ())), precision=precision,
                           preferred_element_type=F32)


def _dot3(a, b, dims=((1,), (0,))):
    a_hi = a.astype(BF16)
    a_lo = (a - a_hi.astype(F32)).astype(BF16)
    b_hi = b.astype(BF16)
    b_lo = (b - b_hi.astype(F32)).astype(BF16)
    return _dot(a_hi, b_hi, dims) + (_dot(a_lo, b_hi, dims) + _dot(a_hi, b_lo, dims))


def _rope128(z, cos, sin_signed, first_half):
    partner = jnp.where(first_half, pltpu.roll(z, LANES - 32, 1), pltpu.roll(z, 32, 1))
    return z * cos + partner * sin_signed


def _inproj_kernel(x_ref, w_ref, cos_ref, sin_ref, rw_ref, q_ref, k_ref, v_ref):
    x = x_ref[...].astype(BF16)
    rw_ref[...] = _dot(x, w_ref[:, :RW_COLS])
    cos = cos_ref[...]
    sin = sin_ref[...]
    lane = lax.broadcasted_iota(jnp.int32, cos.shape, 1)
    first_half = (lane & 32) == 0
    q = _dot(x, w_ref[:, RW_COLS:RW_COLS + DF_W])
    k = _dot(x, w_ref[:, RW_COLS + DF_W:RW_COLS + 2 * DF_W])
    for g in range(DF_W // LANES):
        sl = slice(g * LANES, (g + 1) * LANES)
        q_ref[:, sl] = _rope128(q[:, sl], cos, sin, first_half)
        k_ref[:, sl] = _rope128(k[:, sl], cos, sin, first_half)
    v_ref[...] = _dot(x, w_ref[:, RW_COLS + 2 * DF_W:])


def _inproj(x, w_bf, cos, sin):
    rows, d = x.shape
    in_cols = w_bf.shape[1]
    tm = ROW_TILE
    row = lambda i: (i, 0)
    return pl.pallas_call(
        _inproj_kernel,
        grid=(rows // tm,),
        in_specs=[pl.BlockSpec((tm, d), row),
                  pl.BlockSpec((d, in_cols), lambda i: (0, 0)),
                  pl.BlockSpec((tm, LANES), row),
                  pl.BlockSpec((tm, LANES), row)],
        out_specs=[pl.BlockSpec((tm, RW_COLS), row),
                   pl.BlockSpec((tm, DF_W), row),
                   pl.BlockSpec((tm, DF_W), row),
                   pl.BlockSpec((tm, DF_W), row)],
        out_shape=[jax.ShapeDtypeStruct((rows, RW_COLS), F32),
                   jax.ShapeDtypeStruct((rows, DF_W), F32),
                   jax.ShapeDtypeStruct((rows, DF_W), F32),
                   jax.ShapeDtypeStruct((rows, DF_W), F32)],
        compiler_params=_cparams(("parallel",)),
        name="inproj_rope",
    )(x, w_bf, cos, sin)


def _prep_kernel(t_real, cols_ref, shift0_ref, mu_ref, w0_ref, w2_ref, a0_ref, a2_ref, g2_ref,
                 kk_ref, ka_ref, headsum_ref,
                 r_ref, lw_ref, k_ref, v_ref, kkn_ref, b_ref, g_ref, carry_ref):
    ti = pl.program_id(1)
    tm = cols_ref.shape[0]

    @pl.when(ti == 0)
    def _():
        carry_ref[...] = jnp.broadcast_to(shift0_ref[0], carry_ref.shape)

    x = cols_ref[...]
    rowi = lax.broadcasted_iota(jnp.int32, x.shape, 0)
    prev = jnp.where(rowi == 0, carry_ref[0:1, :], pltpu.roll(x, 1, 0))
    carry_ref[...] = jnp.broadcast_to(x[tm - 1:tm, :], carry_ref.shape)
    mixed = x + (prev - x) * mu_ref[...]
    r = mixed[:, 0:RW_W]
    k = mixed[:, RW_W:2 * RW_W]
    v = mixed[:, 2 * RW_W:3 * RW_W]
    o = 3 * RW_W
    xw = mixed[:, o:o + DECAY_LORA]
    xa = mixed[:, o + DECAY_LORA:o + DECAY_LORA + AAA_LORA]
    xg = mixed[:, o + DECAY_LORA + AAA_LORA:]
    d = w0_ref[...] + _dot3(jnp.tanh(xw), w2_ref[...])
    lw = -jax.nn.sigmoid(d) * math.exp(-0.5)
    a = jax.nn.sigmoid(a0_ref[...] + _dot3(xa, a2_ref[...]))
    g = _dot3(jax.nn.sigmoid(xg), g2_ref[...])
    kk = k * kk_ref[...]
    k2 = k * (1.0 + (a - 1.0) * ka_ref[...])
    ssq = _dot3(kk * kk, headsum_ref[...])
    kkn = kk / jnp.maximum(jnp.sqrt(ssq), 1e-12)
    valid = (ti * tm + lax.broadcasted_iota(jnp.int32, (tm, RW_W), 0)) < t_real
    r_ref[0] = r
    lw_ref[0] = jnp.where(valid, lw, 0.0)
    k_ref[0] = jnp.where(valid, k2, 0.0)
    v_ref[0] = v
    kkn_ref[0] = jnp.where(valid, kkn, 0.0)
    b_ref[0] = jnp.where(valid, kkn * a, 0.0)
    g_ref[0] = g


def _rwkv_prep(cols, bx, tx, shift0, t_real, p):
    tm = min(PREP_TILE, tx)
    blk = lambda b, t: (b, t, 0)
    vec = lambda n: pl.BlockSpec((1, n), lambda b, t: (0, 0))
    mat = lambda r, c: pl.BlockSpec((r, c), lambda b, t: (0, 0))
    outs = [jax.ShapeDtypeStruct((bx, tx, RW_W), F32)] * 7
    return pl.pallas_call(
        functools.partial(_prep_kernel, t_real),
        grid=(bx, tx // tm),
        in_specs=[pl.BlockSpec((tm, RW_COLS), lambda b, t: (b * (tx // tm) + t, 0)),
                  pl.BlockSpec((1, 1, RW_COLS), lambda b, t: (b, 0, 0)),
                  vec(RW_COLS), vec(RW_W), mat(DECAY_LORA, RW_W), vec(RW_W), mat(AAA_LORA, RW_W),
                  mat(GATE_LORA, RW_W), vec(RW_W), vec(RW_W), mat(RW_W, RW_W)],
        out_specs=[pl.BlockSpec((1, tm, RW_W), blk)] * 7,
        out_shape=outs,
        scratch_shapes=[pltpu.VMEM((SUBLANES, RW_COLS), F32)],
        compiler_params=_cparams(("parallel", "arbitrary")),
        name="rwkv_prep",
    )(cols, shift0.reshape(bx, 1, RW_COLS), p["mu_shift"], p["w0"], p["w2_decay"], p["a0"], p["a2"],
      p["g2"], p["k_k"], p["k_a"], p["headsum"])


def _wkv_kernel(r_ref, lw_ref, k_ref, v_ref, kkn_ref, b_ref, g_ref, s0_ref, rk_ref, lnw_ref, lnb_ref,
                out_ref, s_ref):
    ci = pl.program_id(1)
    c = r_ref.shape[1]

    @pl.when(ci == 0)
    def _():
        s_ref[...] = s0_ref[...]

    row = lax.broadcasted_iota(jnp.int32, (c, c), 0)
    col = lax.broadcasted_iota(jnp.int32, (c, c), 1)
    tri = (col <= row).astype(F32)
    nb = r_ref.shape[0]
    a_hat, b_hat, k_hat, r_hat, b_end, k_end, gamma, rkk, v = ([] for _ in range(9))
    for bi in range(nb):
        lw = lw_ref[bi]
        cum = _dot3(tri, lw)
        cum_last = cum[c - 1:c, :]
        g_out = jnp.exp(-cum)
        g_end = jnp.exp(cum_last - cum)
        a_hat.append(-kkn_ref[bi] * jnp.exp(cum - lw))
        b_hat.append(b_ref[bi] * g_out)
        k_hat.append(k_ref[bi] * g_out)
        r_hat.append(r_ref[bi] * jnp.exp(cum))
        b_end.append(b_ref[bi] * g_end)
        k_end.append(k_ref[bi] * g_end)
        gamma.append(jnp.exp(cum_last))
        rkk.append(r_ref[bi] * k_ref[bi] * rk_ref[...])
        v.append(v_ref[bi])

    row2 = lax.broadcasted_iota(jnp.int32, (2 * c, 2 * c), 0)
    col2 = lax.broadcasted_iota(jnp.int32, (2 * c, 2 * c), 1)
    rt = jnp.where(row2 >= c, row2 - c, row2)
    cs = jnp.where(col2 >= c, col2 - c, col2)
    keep = cs < rt + jnp.where(row2 >= c, 1, 0)

    eye = (row == col).astype(F32)
    n_sq = int(math.log2(c))
    probs = [(bi, h) for bi in range(nb) for h in range(RW_HEADS)]
    n = range(len(probs))
    hsl = [slice(h * RW_DH, (h + 1) * RW_DH) for _, h in probs]
    bf = lambda x: x.astype(BF16)
    s_old = [s_ref[bi, h] for bi, h in probs]
    a_h = [a_hat[bi][:, hsl[i]] for i, (bi, _) in enumerate(probs)]
    v_h = [v[bi][:, hsl[i]] for i, (bi, _) in enumerate(probs)]
    r_h = [r_hat[bi][:, hsl[i]] for i, (bi, _) in enumerate(probs)]
    vb = [bf(x) for x in v_h]
    prod = [jnp.where(keep, _dot(bf(jnp.concatenate([a_h[i], r_h[i]], axis=0)),
                                 bf(jnp.concatenate([b_hat[bi][:, hsl[i]], k_hat[bi][:, hsl[i]]], axis=0)),
                                 ((1,), (1,))), 0.0) for i, (bi, _) in enumerate(probs)]
    p = [prod[i][:c, :c] for i in n]
    t = [eye + p[i] for i in n]
    w = [_dot(bf(prod[i][:c, c:]), vb[i]) for i in n]
    for _ in range(n_sq - 1):
        pb = [bf(p[i]) for i in n]
        p = [_dot(pb[i], pb[i]) for i in n]
        t = [t[i] + _dot(bf(p[i]), bf(t[i])) for i in n]
    tb = [bf(t[i]) for i in n]
    u0 = [_dot(tb[i], bf(w[i])) for i in n]
    ta = [_dot(tb[i], bf(a_h[i])) for i in n]
    xs = [_dot(bf(jnp.concatenate([ta[i], r_h[i]], axis=0)), bf(s_old[i]), ((1,), (1,))) for i in n]
    uv = [bf(jnp.concatenate([xs[i][:c] + u0[i], v_h[i]], axis=0)) for i in n]
    s_new = [s_old[i] * gamma[bi][:, hsl[i]]
             + _dot(uv[i], bf(jnp.concatenate([b_end[bi][:, hsl[i]], k_end[bi][:, hsl[i]]], axis=0)),
                    ((0,), (0,))) for i, (bi, _) in enumerate(probs)]
    y = [xs[i][c:] + _dot(bf(prod[i][c:, :]), uv[i]) for i in n]
    for i, (bi, h) in enumerate(probs):
        s_ref[bi, h] = s_new[i]
    for i, (bi, h) in enumerate(probs):
        sl = hsl[i]
        mu = jnp.mean(y[i], axis=-1, keepdims=True)
        yc = y[i] - mu
        var = jnp.mean(yc * yc, axis=-1, keepdims=True)
        yn = yc * lax.rsqrt(var + LNX_EPS) * lnw_ref[:, sl] + lnb_ref[:, sl]
        bonus = jnp.sum(rkk[bi][:, sl], axis=-1, keepdims=True) * v_h[i]
        out_ref[bi, :, sl] = (yn + bonus) * g_ref[bi, :, sl]


def _wkv(prep, s0, p):
    r = prep[0]
    bx, tx, _ = r.shape
    c = WKV_CHUNK
    nb = WKV_BATCH
    blk = pl.BlockSpec((nb, c, RW_W), lambda b, t: (b, t, 0))
    vec = pl.BlockSpec((1, RW_W), lambda b, t: (0, 0))
    st = pl.BlockSpec((nb, RW_HEADS, RW_DH, RW_DH), lambda b, t: (b, 0, 0, 0))
    return pl.pallas_call(
        _wkv_kernel,
        grid=(bx // nb, tx // c),
        in_specs=[blk] * 7 + [st, vec, vec, vec],
        out_specs=[blk, st],
        out_shape=[jax.ShapeDtypeStruct((bx, tx, RW_W), F32),
                   jax.ShapeDtypeStruct((bx, RW_HEADS, RW_DH, RW_DH), F32)],
        compiler_params=_cparams(("parallel", "arbitrary")),
        name="wkv_chunk",
    )(*prep, s0, p["r_k"], p["lnx_w"], p["lnx_b"])


def _attn_kernel(lam_init, lam_ref, q_ref, k_ref, v_ref, sw_ref, o_ref, kt_ref, vx_ref, m_ref, acc_ref):
    qi = pl.program_id(2)
    tq = q_ref.shape[0]

    @pl.when(qi == 0)
    def _():
        kt_ref[...] = k_ref[...].T.astype(BF16)
        vx_ref[:, :DF_DV] = v_ref[...].astype(BF16)
        vx_ref[:, DF_DV:] = jnp.ones((vx_ref.shape[0], DF_DV), BF16)

    q = (q_ref[...] * (DF_DK ** -0.5)).astype(BF16)
    qc = [q[:, :DF_DK], q[:, DF_DK:]]
    m_ref[...] = jnp.full(m_ref.shape, -jnp.inf, F32)
    acc_ref[...] = jnp.zeros(acc_ref.shape, F32)
    def block(start, width, masked):
        vblk = vx_ref[pl.ds(start, width), :]
        s = [_dot(qc[c], kt_ref[c * DF_DK:(c + 1) * DF_DK, pl.ds(start, width)]) for c in range(2)]
        if masked:
            rowi = lax.broadcasted_iota(jnp.int32, (tq, width), 0)
            coli = lax.broadcasted_iota(jnp.int32, (tq, width), 1)
            causal = coli <= rowi + (width - tq)
            s = [jnp.where(causal, s[c], -jnp.inf) for c in range(2)]
        m_old = [m_ref[c] for c in range(2)]
        m_new = [jnp.maximum(m_old[c], jnp.max(s[c], axis=-1, keepdims=True)) for c in range(2)]
        pr = [jnp.exp(s[c] - m_new[c]).astype(BF16) for c in range(2)]
        pv = [_dot(pr[c], vblk) for c in range(2)]
        for c in range(2):
            acc_ref[c] = jnp.exp(m_old[c] - m_new[c]) * acc_ref[c] + pv[c]
            m_ref[c] = m_new[c]

    def pair(pj, carry):
        block(pl.multiple_of(pj * (2 * tq), 2 * tq), 2 * tq, False)
        return carry

    lax.fori_loop(0, qi // 2, pair, 0)

    @pl.when(qi % 2 == 1)
    def _():
        block(pl.multiple_of((qi - 1) * tq, tq), 2 * tq, True)

    @pl.when(qi % 2 == 0)
    def _():
        block(pl.multiple_of(qi * tq, tq), tq, True)
    o = (acc_ref[0, :, :DF_DV] / acc_ref[0, :, DF_DV:]
         - lam_ref[0] * (acc_ref[1, :, :DF_DV] / acc_ref[1, :, DF_DV:]))
    o = o * lax.rsqrt(jnp.mean(o * o, axis=-1, keepdims=True) + SUBLN_EPS)
    o_ref[...] = o * sw_ref[...] * (1.0 - lam_init)


def _attn_prompt(q, k, v, b, tp, lam, subln_w, lam_init):
    tq = ATT_TILE
    nq = tp // tq
    return pl.pallas_call(
        functools.partial(_attn_kernel, lam_init),
        grid=(b, DF_HEADS, nq),
        in_specs=[pl.BlockSpec(memory_space=pltpu.SMEM),
                  pl.BlockSpec((tq, DF_DV), lambda bi, h, i: (bi * nq + i, h)),
                  pl.BlockSpec((tp, DF_DV), lambda bi, h, i: (bi, h)),
                  pl.BlockSpec((tp, DF_DV), lambda bi, h, i: (bi, h)),
                  pl.BlockSpec((1, DF_DV), lambda bi, h, i: (0, h))],
        out_specs=pl.BlockSpec((tq, DF_DV), lambda bi, h, i: (bi * nq + i, h)),
        out_shape=jax.ShapeDtypeStruct((b * tp, DF_W), F32),
        scratch_shapes=[pltpu.VMEM((2 * DF_DK, tp), BF16), pltpu.VMEM((tp, 2 * DF_DV), BF16),
                        pltpu.VMEM((2, tq, 1), F32), pltpu.VMEM((2, tq, 2 * DF_DV), F32)],
        compiler_params=_cparams(("parallel", "parallel", "arbitrary")),
        name="diff_attn_prompt",
    )(lam, q, k, v, subln_w)


def _decode_kernel(lam_init, n_pg, pt_ref, lam_ref, q_ref, kn_ref, vn_ref, *refs):
    ck_refs = refs[:n_pg]
    cv_refs = refs[n_pg:2 * n_pg]
    sw_ref, o_ref, qrow_ref, m_ref, l_ref, acc_ref = refs[2 * n_pg:]
    pi = pl.program_id(1)
    n_pages = pl.num_programs(1)
    nq = 2 * DF_HEADS
    page = ck_refs[0].shape[2]

    @pl.when(pi == 0)
    def _():
        rowi = lax.broadcasted_iota(jnp.int32, (nq, DF_W), 0)
        coli = lax.broadcasted_iota(jnp.int32, (nq, DF_W), 1)
        qb = jnp.broadcast_to(q_ref[0] * (DF_DK ** -0.5), (nq, DF_W))
        qrow_ref[...] = jnp.where(jnp.right_shift(coli, int(math.log2(DF_DK))) == rowi, qb, 0.0)
        m_ref[...] = jnp.full(m_ref.shape, -jnp.inf, F32)
        l_ref[...] = jnp.zeros(l_ref.shape, F32)
        acc_ref[...] = jnp.zeros(acc_ref.shape, F32)

    qrow = qrow_ref[...]
    qb = qrow.astype(BF16)
    s = jnp.concatenate([_dot(qb, ck_refs[g][0].astype(BF16)) for g in range(n_pg)], axis=1)
    m_old = m_ref[...]
    m_new = jnp.maximum(m_old, jnp.max(s, axis=-1, keepdims=True))
    alpha = jnp.exp(m_old - m_new)
    pr = jnp.exp(s - m_new)
    l_ref[...] = alpha * l_ref[...] + jnp.sum(pr, axis=-1, keepdims=True)
    prb = pr.astype(BF16)
    for h in range(DF_HEADS):
        pv = None
        for g in range(n_pg):
            v_gh = cv_refs[g][0, pl.ds(h, page, stride=DF_HEADS), :].astype(BF16)
            d = _dot(prb[:, g * page:(g + 1) * page], v_gh)
            pv = d if pv is None else pv + d
        sl = slice(h * DF_DV, (h + 1) * DF_DV)
        acc_ref[:, sl] = alpha * acc_ref[:, sl] + pv
    m_ref[...] = m_new

    @pl.when(pi == n_pages - 1)
    def _():
        s_self = jnp.sum(qrow * kn_ref[0], axis=-1, keepdims=True)
        m_old = m_ref[...]
        m_fin = jnp.maximum(m_old, s_self)
        alpha = jnp.exp(m_old - m_fin)
        p_self = jnp.exp(s_self - m_fin)
        l_fin = alpha * l_ref[...] + p_self
        acc = (alpha * acc_ref[...] + p_self * vn_ref[0]) / l_fin
        for h in range(DF_HEADS):
            sl = slice(h * DF_DV, (h + 1) * DF_DV)
            o = acc[2 * h:2 * h + 1, sl] - lam_ref[0] * acc[2 * h + 1:2 * h + 2, sl]
            o = o * lax.rsqrt(jnp.mean(o * o, axis=-1, keepdims=True) + SUBLN_EPS)
            o_ref[0, :, sl] = o * sw_ref[:, sl] * (1.0 - lam_init)


def _attn_sample(q, k_new, v_new, cache_k, cache_v, page_table, lam, subln_w, lam_init):
    bs = q.shape[0]
    n_pages = page_table.shape[1]
    n_pool, page = cache_k.shape[:2]
    ck = jnp.transpose(cache_k, (0, 2, 3, 4, 1)).reshape(n_pool, DF_W, page)
    cv = cache_v.reshape(n_pool, page * DF_HEADS, DF_DV)
    n_pg = math.gcd(DEC_PAGES, n_pages)
    row = pl.BlockSpec((1, 1, DF_W), lambda b, p, pt: (b, 0, 0))
    ck_specs = [pl.BlockSpec((1, DF_W, page), lambda b, p, pt, g=g: (pt[b, p * n_pg + g], 0, 0))
                for g in range(n_pg)]
    cv_specs = [pl.BlockSpec((1, page * DF_HEADS, DF_DV), lambda b, p, pt, g=g: (pt[b, p * n_pg + g], 0, 0))
                for g in range(n_pg)]
    nq = 2 * DF_HEADS
    grid_spec = pltpu.PrefetchScalarGridSpec(
        num_scalar_prefetch=1,
        grid=(bs, n_pages // n_pg),
        in_specs=[pl.BlockSpec(memory_space=pltpu.SMEM), row, row, row] + ck_specs + cv_specs
                 + [pl.BlockSpec((1, DF_W), lambda b, p, pt: (0, 0))],
        out_specs=row,
        scratch_shapes=[pltpu.VMEM((nq, DF_W), F32), pltpu.VMEM((nq, 1), F32),
                        pltpu.VMEM((nq, 1), F32), pltpu.VMEM((nq, DF_W), F32)],
    )
    out = pl.pallas_call(
        functools.partial(_decode_kernel, lam_init, n_pg),
        grid_spec=grid_spec,
        out_shape=jax.ShapeDtypeStruct((bs, 1, DF_W), F32),
        compiler_params=_cparams(("parallel", "arbitrary")),
        name="diff_attn_paged",
    )(page_table, lam, q.reshape(bs, 1, DF_W), k_new.reshape(bs, 1, DF_W), v_new.reshape(bs, 1, DF_W),
      *([ck] * n_pg), *([cv] * n_pg), subln_w)
    return out.reshape(bs, DF_W)


def _layer_norm(x, w, b):
    mu = jnp.mean(x, axis=-1, keepdims=True)
    xc = x - mu
    var = jnp.mean(xc * xc, axis=-1, keepdims=True)
    return xc * lax.rsqrt(var + LN_EPS) * w + b


def _outproj_kernel(n_main, x_ref, rw_ref, o_ref, rwt_ref, ot_ref, w_ref, lnw_ref, lnb_ref, y_ref, yt_ref):
    main = pl.program_id(0) < n_main
    rw = jnp.where(main, rw_ref[...], rwt_ref[...]).astype(BF16)
    o = jnp.where(main, o_ref[...], ot_ref[...]).astype(BF16)
    mix = _dot(rw, w_ref[:RW_W, :]) + _dot(o, w_ref[RW_W:, :])
    y = _layer_norm(DEEP_ALPHA * x_ref[...] + mix, lnw_ref[...], lnb_ref[...])
    y_ref[...] = y
    yt_ref[...] = y.T.astype(BF16)


def _outproj(x, rw_main, o_main, rw_tail, o_tail, w_bf, ln_w, ln_b):
    rows, d = x.shape
    tm = ROW_TILE
    n_main = rw_main.shape[0] // tm
    assert rw_main.shape[0] % tm == 0 and rows == (n_main + 1) * tm and rw_tail.shape[0] == tm
    row = lambda i: (i, 0)
    main = lambda i: (jnp.minimum(i, n_main - 1), 0)
    tail = lambda i: (0, 0)
    return pl.pallas_call(
        functools.partial(_outproj_kernel, n_main),
        grid=(rows // tm,),
        in_specs=[pl.BlockSpec((tm, d), row), pl.BlockSpec((tm, RW_W), main), pl.BlockSpec((tm, DF_W), main),
                  pl.BlockSpec((tm, RW_W), tail), pl.BlockSpec((tm, DF_W), tail),
                  pl.BlockSpec(w_bf.shape, lambda i: (0, 0)),
                  pl.BlockSpec((1, d), lambda i: (0, 0)), pl.BlockSpec((1, d), lambda i: (0, 0))],
        out_specs=[pl.BlockSpec((tm, d), row), pl.BlockSpec((d, tm), lambda i: (0, i))],
        out_shape=[jax.ShapeDtypeStruct((rows, d), F32), jax.ShapeDtypeStruct((d, rows), BF16)],
        compiler_params=_cparams(("parallel",)),
        name="outproj_ln",
    )(x, rw_main, o_main, rw_tail, o_tail, w_bf, ln_w, ln_b)


def _peer_pairs():
    n = PK_TOPK + 1
    return [(i, j) for i in range(n) for j in range(n) if (i + 1) * (j + 1) <= n]


def _sort_pairs(n):
    pairs = []
    p = 1
    while p < n:
        k = p
        while k >= 1:
            for j in range(k % p, n - k, 2 * k):
                for i in range(min(k, n - j - k)):
                    if (i + j) // (2 * p) == (i + j + k) // (2 * p):
                        pairs.append((i + j, i + j + k))
            k //= 2
        p *= 2
    return pairs


def _extract_top(cur, n, n_pad):
    rows, cols = cur.shape
    n_tiles = rows // SUBLANES
    size = 1 << (n_tiles - 1).bit_length()
    neg = jnp.full((SUBLANES, cols), -jnp.inf, F32)
    t = [cur[i * SUBLANES:(i + 1) * SUBLANES, :] for i in range(n_tiles)] + [neg] * (size - n_tiles)
    for i, j in _sort_pairs(size):
        t[i], t[j] = jnp.maximum(t[i], t[j]), jnp.minimum(t[i], t[j])
    vals = []
    rowi = lax.broadcasted_iota(jnp.int32, (n_pad, cols), 0)
    stacked = jnp.zeros((n_pad, cols), F32)
    for k in range(n):
        m = jnp.max(t[0], axis=0, keepdims=True)
        vals.append(m)
        stacked = jnp.where(rowi == k, m, stacked)
        hit = t[0] == m
        keep = min(len(t), n - 1 - k)
        t = [jnp.where(hit, t[l + 1] if l + 1 < len(t) else neg, t[l]) for l in range(keep)]
    return vals, stacked


def _peer_score_kernel(x_ref, wq_ref, sk_ref, sela_ref, selb_ref,
                       e1_ref, e2_ref, psi_ref, qt_ref):
    n_top = PK_TOPK + 1
    half = PK_DQ // 2
    tt = x_ref.shape[0]
    x = x_ref[...]
    x_hi = x.astype(BF16)
    x_lo = (x - x_hi.astype(F32)).astype(BF16)
    nt = ((1,), (1,))
    qt_ref[...] = (_dot(wq_ref[0], x_hi, nt)
                   + (_dot(wq_ref[1], x_hi, nt) + _dot(wq_ref[0], x_lo, nt)))
    n_cand = sela_ref.shape[0]
    n_pad = sela_ref.shape[1]
    cand_row = lax.broadcasted_iota(jnp.int32, (n_cand, tt), 0)
    n_pairs = len(_peer_pairs())
    for h in range(PK_HEADS):
        s1 = _dot3(sk_ref[h, 0], qt_ref[pl.ds(h * PK_DQ, half), :])
        s2 = _dot3(sk_ref[h, 1], qt_ref[pl.ds(h * PK_DQ + half, half), :])
        top1, a_st = _extract_top(s1, n_top, n_pad)
        top2, b_st = _extract_top(s2, n_top, n_pad)
        cand = _dot(sela_ref[...], a_st, precision=HI) + _dot(selb_ref[...], b_st, precision=HI)
        cand = jnp.where(cand_row < n_pairs, cand, -jnp.inf)
        ctop, _ = _extract_top(cand, n_top, n_pad)
        cmax = ctop[0]
        theta = 0.5 * (ctop[PK_TOPK - 1] + ctop[PK_TOPK])
        z = jnp.sum(jnp.where(cand >= theta, jnp.exp(cand - cmax), 0.0), axis=0, keepdims=True)
        e1_ref[h] = jnp.exp(s1 - top1[0]) * (0.5 / z)
        e2_ref[h] = jnp.exp(s2 - top2[0])
        psi_ref[h] = jnp.exp(theta - s1 - top2[0])


def _peer_scores(x1, wq_t, subkeys, sel_a, sel_b):
    rows, d = x1.shape
    tt = PEER_TOK
    out = jax.ShapeDtypeStruct((PK_HEADS, N_KEYS, rows), F32)
    ospec = pl.BlockSpec((PK_HEADS, N_KEYS, tt), lambda i: (0, 0, i))
    return pl.pallas_call(
        _peer_score_kernel,
        grid=(rows // tt,),
        in_specs=[pl.BlockSpec((tt, d), lambda i: (i, 0)),
                  pl.BlockSpec(wq_t.shape, lambda i: (0, 0, 0)),
                  pl.BlockSpec(subkeys.shape, lambda i: (0, 0, 0, 0)),
                  pl.BlockSpec(sel_a.shape, lambda i: (0, 0)),
                  pl.BlockSpec(sel_b.shape, lambda i: (0, 0))],
        out_specs=[ospec] * 3,
        out_shape=[out] * 3,
        scratch_shapes=[pltpu.VMEM((PK_HEADS * PK_DQ, tt), F32)],
        compiler_params=_cparams(("parallel",)),
        name="peer_scores",
    )(x1, wq_t, subkeys, sel_a, sel_b)


def _peer_expert_kernel(x_ref, xt_ref, u_ref, vt_ref, e1_ref, e2_ref, psi_ref, lnw_ref, lnb_ref,
                        y_ref, wt_ref, acc_ref, act_ref, pb_ref, eb_ref):
    ei = pl.program_id(1)
    te = u_ref.shape[0]
    tt = x_ref.shape[0]

    @pl.when(ei == 0)
    def _():
        acc_ref[...] = jnp.zeros(acc_ref.shape, F32)

    n_rows = te // N_KEYS
    k_rows = PEER_KSPLIT // N_KEYS
    n_sub = N_KEYS // SUBLANES
    for j in range(n_rows):
        i1 = ei * n_rows + j
        for h in range(PK_HEADS):
            pb_ref[j, h] = jnp.broadcast_to(psi_ref[h, pl.ds(i1, 1), :], (SUBLANES, tt))
            eb_ref[j, h] = jnp.broadcast_to(e1_ref[h, pl.ds(i1, 1), :], (SUBLANES, tt))
    n_grp = n_rows // k_rows

    def first_product(g):
        act_ref[g % 2] = _dot(u_ref[g * PEER_KSPLIT:(g + 1) * PEER_KSPLIT, :], xt_ref[...])

    def second_product(g):
        ks = slice(g * PEER_KSPLIT, (g + 1) * PEER_KSPLIT)
        return _dot(vt_ref[:, ks], wt_ref[ks, :])

    def gate_row(j):
        rows = slice(j * N_KEYS, (j + 1) * N_KEYS)
        arow = slice((j % k_rows) * N_KEYS, (j % k_rows + 1) * N_KEYS)
        for tg in range(tt // LANES):
            ls = slice(tg * LANES, (tg + 1) * LANES)
            gate = jnp.zeros((n_sub, SUBLANES, LANES), F32)
            for h in range(PK_HEADS):
                e2 = e2_ref[h, :, ls].reshape(n_sub, SUBLANES, LANES)
                gate = gate + jnp.where(e2 >= pb_ref[j, h, :, ls][None], eb_ref[j, h, :, ls][None] * e2, 0.0)
            a = act_ref[(j // k_rows) % 2, arow, ls]
            gate = gate.reshape(N_KEYS, LANES)
            wt_ref[rows, ls] = (gate * (a + a * lax.erf(a * (2.0 ** -0.5)))).astype(BF16)

    first_product(0)
    part = None
    for g in range(n_grp):
        if g + 1 < n_grp:
            first_product(g + 1)
        for r in range(k_rows):
            gate_row(g * k_rows + r)
            if r == 0 and g > 0:
                d = second_product(g - 1)
                part = d if part is None else part + d
    d = second_product(n_grp - 1)
    acc_ref[...] += d if part is None else part + d

    @pl.when(ei == pl.num_programs(1) - 1)
    def _():
        ch = acc_ref[...].T
        y_ref[...] = _layer_norm(DEEP_ALPHA * x_ref[...] + ch, lnw_ref[...], lnb_ref[...])


def _peer_experts(x1, x1_t, u_bf, vt_bf, e1, e2, psi, ln_w, ln_b):
    rows, d = x1.shape
    n_exp = u_bf.shape[0]
    tt = PEER_TOK
    te = PEER_EXP
    tok = lambda i, e: (i, 0)
    sc = pl.BlockSpec((PK_HEADS, N_KEYS, tt), lambda i, e: (0, 0, i))
    vec = pl.BlockSpec((1, d), lambda i, e: (0, 0))
    return pl.pallas_call(
        _peer_expert_kernel,
        grid=(rows // tt, n_exp // te),
        in_specs=[pl.BlockSpec((tt, d), tok), pl.BlockSpec((d, tt), lambda i, e: (0, i)),
                  pl.BlockSpec((te, d), lambda i, e: (e, 0)),
                  pl.BlockSpec((d, te), lambda i, e: (0, e)),
                  sc, sc, sc, vec, vec],
        out_specs=pl.BlockSpec((tt, d), tok),
        out_shape=jax.ShapeDtypeStruct((rows, d), F32),
        scratch_shapes=[pltpu.VMEM((te, tt), BF16), pltpu.VMEM((d, tt), F32),
                        pltpu.VMEM((2, PEER_KSPLIT, tt), F32),
                        pltpu.VMEM((te // N_KEYS, PK_HEADS, SUBLANES, tt), F32),
                        pltpu.VMEM((te // N_KEYS, PK_HEADS, SUBLANES, tt), F32)],
        compiler_params=_cparams(("parallel", "arbitrary")),
        name="peer_experts",
    )(x1, x1_t, u_bf, vt_bf, e1, e2, psi, ln_w, ln_b)


def _rope_tables(pos):
    half = DF_DK // 2
    inv = ROPE_THETA ** (-jnp.arange(half, dtype=F32) / half)
    ang = pos.astype(F32)[:, None] * inv[None, :]
    cos, sin = jnp.cos(ang), jnp.sin(ang)
    cos128 = jnp.concatenate([cos, cos, cos, cos], axis=1)
    sin128 = jnp.concatenate([-sin, sin, -sin, sin], axis=1)
    return cos128, sin128


def _round_up(n, m):
    return -(-n // m) * m


def kernel(x_prompt, x_sample, cache_k, cache_v, page_table, state_wkv, state_shift, meta_tokens, w_in,
           mu_shift, w0, w2_decay, a0, a2, g2, k_k, k_a, r_k, lnx_w, lnx_b, lam_q1, lam_k1, lam_q2, lam_k2,
           subln_w, w_out, ln1_w, ln1_b, pk_wq, pk_subkeys, peer_u, peer_v, ln2_w, ln2_b):
    assert w_in.shape[0] == DEPTH == 1
    bp, seq, d = x_prompt.shape
    bs, dec_seq, _ = x_sample.shape
    assert dec_seq == 1
    t_prompt = N_META + seq
    tp = _round_up(t_prompt, math.lcm(ATT_TILE, PREP_TILE, WKV_CHUNK))
    n_prompt_rows = bp * tp
    rows = _round_up(n_prompt_rows + bs, ROW_TILE)
    page = cache_k.shape[2]
    past = page_table.shape[1] * page
    lam_init = 0.8 - 0.6 * math.exp(-0.3 * 0)
    row2 = lambda a: a.reshape(1, -1)

    meta = meta_tokens.astype(x_prompt.dtype)
    pad = jnp.zeros((tp - t_prompt, d), x_prompt.dtype)
    pieces = [p for b in range(bp) for p in (meta, x_prompt[b], pad)]
    x_all = jnp.concatenate(pieces + [x_sample.reshape(bs, d),
                                      jnp.zeros((rows - n_prompt_rows - bs, d), x_prompt.dtype)], axis=0)
    pos = jnp.concatenate([jnp.tile(jnp.arange(tp), bp), jnp.full((rows - n_prompt_rows,), past)])
    cos, sin = _rope_tables(pos)

    rw_cols, q, k, v = _inproj(x_all, w_in[0].astype(BF16), cos, sin)

    sm = lambda a: a[n_prompt_rows:n_prompt_rows + bs]

    hs = jnp.arange(RW_W) // RW_DH
    rw_params = {
        "mu_shift": row2(mu_shift[0]), "w0": row2(w0[0]), "w2_decay": w2_decay[0], "a0": row2(a0[0]),
        "a2": a2[0], "g2": g2[0], "k_k": row2(k_k[0]), "k_a": row2(k_a[0]),
        "headsum": (hs[:, None] == hs[None, :]).astype(F32),
        "r_k": row2(r_k[0]), "lnx_w": row2(lnx_w[0]), "lnx_b": row2(lnx_b[0]),
    }
    prep_p = _rwkv_prep(rw_cols, bp, tp, jnp.zeros((bp, RW_COLS), F32), t_prompt, rw_params)
    rw_out_p, wkv_p = _wkv(prep_p, jnp.zeros((bp, RW_HEADS, RW_DH, RW_DH), F32), rw_params)
    ts = WKV_CHUNK
    rw_s = jnp.concatenate([sm(rw_cols)[:, None, :], jnp.zeros((bs, ts - 1, RW_COLS), F32)], axis=1)
    prep_s = _rwkv_prep(rw_s.reshape(bs * ts, RW_COLS), bs, ts, state_shift[0], dec_seq, rw_params)
    rw_out_s, wkv_s = _wkv(prep_s, state_wkv[0], rw_params)

    f32 = lambda t: t.astype(F32)
    lam = (jnp.exp(jnp.sum(f32(lam_q1[0]) * f32(lam_k1[0])))
           - jnp.exp(jnp.sum(f32(lam_q2[0]) * f32(lam_k2[0]))) + lam_init).reshape(1)
    sw = row2(subln_w[0])
    o_p = _attn_prompt(q, k, v, bp, tp, lam, sw, lam_init)
    o_s = _attn_sample(sm(q), sm(k), sm(v), cache_k[0], cache_v[0], page_table, lam, sw, lam_init)

    tail = jnp.zeros((rows - n_prompt_rows - bs, RW_W), F32)
    x1, x1_t = _outproj(x_all, rw_out_p.reshape(n_prompt_rows, RW_W), o_p,
                        jnp.concatenate([rw_out_s[:, 0, :], tail], axis=0), jnp.concatenate([o_s, tail], axis=0),
                        w_out[0].astype(BF16), row2(ln1_w[0]), row2(ln1_b[0]))

    pairs = _peer_pairs()
    n_cand = _round_up(len(pairs), SUBLANES)
    n_pad = _round_up(PK_TOPK + 1, SUBLANES)
    ia = jnp.array([p[0] for p in pairs] + [0] * (n_cand - len(pairs)))
    ib = jnp.array([p[1] for p in pairs] + [0] * (n_cand - len(pairs)))
    sel_a = (ia[:, None] == jnp.arange(n_pad)[None, :]).astype(F32)
    sel_b = (ib[:, None] == jnp.arange(n_pad)[None, :]).astype(F32)
    wq_t = pk_wq[0].T.astype(F32)
    wq_hi = wq_t.astype(BF16)
    wq_lo = (wq_t - wq_hi.astype(F32)).astype(BF16)
    e1, e2, psi = _peer_scores(x1, jnp.stack([wq_hi, wq_lo]), pk_subkeys[0], sel_a, sel_b)
    y = _peer_experts(x1, x1_t, peer_u[0].astype(BF16), peer_v[0].T.astype(BF16),
                      e1, e2, psi, row2(ln2_w[0]), row2(ln2_b[0]))

    seq = lambda a, lo, hi: jnp.stack([a[b * tp + lo:b * tp + hi] for b in range(bp)])
    y_prompt = seq(y, N_META, t_prompt)
    y_sample = sm(y).reshape(bs, dec_seq, d)
    k_p = seq(k, 0, t_prompt).reshape(1, bp, t_prompt, DF_HEADS, 2, DF_DK)
    v_p = seq(v, 0, t_prompt).reshape(1, bp, t_prompt, DF_HEADS, DF_DV)
    shift_p = seq(rw_cols, t_prompt - 1, t_prompt).reshape(1, bp, RW_COLS)
    k_s = sm(k).reshape(1, bs, dec_seq, DF_HEADS, 2, DF_DK)
    v_s = sm(v).reshape(1, bs, dec_seq, DF_HEADS, DF_DV)
    shift_s = sm(rw_cols)[None]
    return (y_prompt, y_sample, k_p, v_p, wkv_p[None], shift_p, k_s, v_s, wkv_s[None], shift_s)
```

```python
import functools
import math

import jax
import jax.numpy as jnp
from jax import lax
from jax.experimental import pallas as pl
from jax.experimental.pallas import tpu as pltpu

F32 = jnp.float32
BF16 = jnp.bfloat16

N_META = 16
RW_HEADS = 8
RW_DH = 64
RW_W = RW_HEADS * RW_DH
DECAY_LORA = 64
AAA_LORA = 64
GATE_LORA = 128
RW_COLS = 3 * RW_W + DECAY_LORA + AAA_LORA + GATE_LORA
LNX_EPS = 64e-5
DF_HEADS = 4
DF_DK = 64
DF_DV = 2 * DF_DK
DF_W = DF_HEADS * DF_DV
ROPE_THETA = 10000.0
SUBLN_EPS = 1e-5
LN_EPS = 1e-5
PK_HEADS = 8
N_KEYS = 128
PK_DQ = 128
PK_TOPK = 16
DEPTH = 1
DEEP_ALPHA = (2 * DEPTH) ** 0.25

LANES = 128
SUBLANES = 8
VMEM_LIMIT = 56 * 1024 * 1024

ROW_TILE = 512
PREP_TILE = 128
WKV_CHUNK = 64
WKV_BATCH = 4
ATT_TILE = 384
DEC_PAGES = 16
PEER_TOK = 512
PEER_EXP = 2048
PEER_KSPLIT = 512
HI = lax.Precision.HIGHEST


def _cparams(sem):
    return pltpu.CompilerParams(dimension_semantics=sem, vmem_limit_bytes=VMEM_LIMIT)


def _dot(a, b, dims=((1,), (0,)), precision=None):
    return lax.dot_general(a, b, (dims, ((), ())), precision=precision,
                           preferred_element_type=F32)


def _dot3(a, b, dims=((1,), (0,))):
    a_hi = a.astype(BF16)
    a_lo = (a - a_hi.astype(F32)).astype(BF16)
    b_hi = b.astype(BF16)
    b_lo = (b - b_hi.astype(F32)).astype(BF16)
    return _dot(a_hi, b_hi, dims) + (_dot(a_lo, b_hi, dims) + _dot(a_hi, b_lo, dims))


def _rope128(z, cos, sin_signed, first_half):
    partner = jnp.where(first_half, pltpu.roll(z, LANES - 32, 1), pltpu.roll(z, 32, 1))
    return z * cos + partner * sin_signed


def _inproj_kernel(x_ref, w_ref, cos_ref, sin_ref, rw_ref, q_ref, k_ref, v_ref):
    x = x_ref[...].astype(BF16)
    rw_ref[...] = _dot(x, w_ref[:, :RW_COLS])
    cos = cos_ref[...]
    sin = sin_ref[...]
    lane = lax.broadcasted_iota(jnp.int32, cos.shape, 1)
    first_half = (lane & 32) == 0
    q = _dot(x, w_ref[:, RW_COLS:RW_COLS + DF_W])
    k = _dot(x, w_ref[:, RW_COLS + DF_W:RW_COLS + 2 * DF_W])
    for g in range(DF_W // LANES):
        sl = slice(g * LANES, (g + 1) * LANES)
        q_ref[:, sl] = _rope128(q[:, sl], cos, sin, first_half)
        k_ref[:, sl] = _rope128(k[:, sl], cos, sin, first_half)
    v_ref[...] = _dot(x, w_ref[:, RW_COLS + 2 * DF_W:])


def _inproj(x, w_bf, cos, sin):
    rows, d = x.shape
    in_cols = w_bf.shape[1]
    tm = ROW_TILE
    row = lambda i: (i, 0)
    return pl.pallas_call(
        _inproj_kernel,
        grid=(rows // tm,),
        in_specs=[pl.BlockSpec((tm, d), row),
                  pl.BlockSpec((d, in_cols), lambda i: (0, 0)),
                  pl.BlockSpec((tm, LANES), row),
                  pl.BlockSpec((tm, LANES), row)],
        out_specs=[pl.BlockSpec((tm, RW_COLS), row),
                   pl.BlockSpec((tm, DF_W), row),
                   pl.BlockSpec((tm, DF_W), row),
                   pl.BlockSpec((tm, DF_W), row)],
        out_shape=[jax.ShapeDtypeStruct((rows, RW_COLS), F32),
                   jax.ShapeDtypeStruct((rows, DF_W), F32),
                   jax.ShapeDtypeStruct((rows, DF_W), F32),
                   jax.ShapeDtypeStruct((rows, DF_W), F32)],
        compiler_params=_cparams(("parallel",)),
        name="inproj_rope",
    )(x, w_bf, cos, sin)


def _prep_kernel(t_real, cols_ref, shift0_ref, mu_ref, w0_ref, w2_ref, a0_ref, a2_ref, g2_ref,
                 kk_ref, ka_ref, headsum_ref,
                 r_ref, lw_ref, k_ref, v_ref, kkn_ref, b_ref, g_ref, carry_ref):
    ti = pl.program_id(1)
    tm = cols_ref.shape[0]

    @pl.when(ti == 0)
    def _():
        carry_ref[...] = jnp.broadcast_to(shift0_ref[0], carry_ref.shape)

    x = cols_ref[...]
    rowi = lax.broadcasted_iota(jnp.int32, x.shape, 0)
    prev = jnp.where(rowi == 0, carry_ref[0:1, :], pltpu.roll(x, 1, 0))
    carry_ref[...] = jnp.broadcast_to(x[tm - 1:tm, :], carry_ref.shape)
    mixed = x + (prev - x) * mu_ref[...]
    r = mixed[:, 0:RW_W]
    k = mixed[:, RW_W:2 * RW_W]
    v = mixed[:, 2 * RW_W:3 * RW_W]
    o = 3 * RW_W
    xw = mixed[:, o:o + DECAY_LORA]
    xa = mixed[:, o + DECAY_LORA:o + DECAY_LORA + AAA_LORA]
    xg = mixed[:, o + DECAY_LORA + AAA_LORA:]
    d = w0_ref[...] + _dot3(jnp.tanh(xw), w2_ref[...])
    lw = -jax.nn.sigmoid(d) * math.exp(-0.5)
    a = jax.nn.sigmoid(a0_ref[...] + _dot3(xa, a2_ref[...]))
    g = _dot3(jax.nn.sigmoid(xg), g2_ref[...])
    kk = k * kk_ref[...]
    k2 = k * (1.0 + (a - 1.0) * ka_ref[...])
    ssq = _dot3(kk * kk, headsum_ref[...])
    kkn = kk / jnp.maximum(jnp.sqrt(ssq), 1e-12)
    valid = (ti * tm + lax.broadcasted_iota(jnp.int32, (tm, RW_W), 0)) < t_real
    r_ref[0] = r
    lw_ref[0] = jnp.where(valid, lw, 0.0)
    k_ref[0] = jnp.where(valid, k2, 0.0)
    v_ref[0] = v
    kkn_ref[0] = jnp.where(valid, kkn, 0.0)
    b_ref[0] = jnp.where(valid, kkn * a, 0.0)
    g_ref[0] = g


def _rwkv_prep(cols, bx, tx, shift0, t_real, p):
    tm = min(PREP_TILE, tx)
    blk = lambda b, t: (b, t, 0)
    vec = lambda n: pl.BlockSpec((1, n), lambda b, t: (0, 0))
    mat = lambda r, c: pl.BlockSpec((r, c), lambda b, t: (0, 0))
    outs = [jax.ShapeDtypeStruct((bx, tx, RW_W), F32)] * 7
    return pl.pallas_call(
        functools.partial(_prep_kernel, t_real),
        grid=(bx, tx // tm),
        in_specs=[pl.BlockSpec((tm, RW_COLS), lambda b, t: (b * (tx // tm) + t, 0)),
                  pl.BlockSpec((1, 1, RW_COLS), lambda b, t: (b, 0, 0)),
                  vec(RW_COLS), vec(RW_W), mat(DECAY_LORA, RW_W), vec(RW_W), mat(AAA_LORA, RW_W),
                  mat(GATE_LORA, RW_W), vec(RW_W), vec(RW_W), mat(RW_W, RW_W)],
        out_specs=[pl.BlockSpec((1, tm, RW_W), blk)] * 7,
        out_shape=outs,
        scratch_shapes=[pltpu.VMEM((SUBLANES, RW_COLS), F32)],
        compiler_params=_cparams(("parallel", "arbitrary")),
        name="rwkv_prep",
    )(cols, shift0.reshape(bx, 1, RW_COLS), p["mu_shift"], p["w0"], p["w2_decay"], p["a0"], p["a2"],
      p["g2"], p["k_k"], p["k_a"], p["headsum"])


def _wkv_kernel(r_ref, lw_ref, k_ref, v_ref, kkn_ref, b_ref, g_ref, s0_ref, rk_ref, lnw_ref, lnb_ref,
                out_ref, s_ref):
    ci = pl.program_id(1)
    c = r_ref.shape[1]

    @pl.when(ci == 0)
    def _():
        s_ref[...] = s0_ref[...]

    row = lax.broadcasted_iota(jnp.int32, (c, c), 0)
    col = lax.broadcasted_iota(jnp.int32, (c, c), 1)
    tri = (col <= row).astype(F32)
    nb = r_ref.shape[0]
    a_hat, b_hat, k_hat, r_hat, b_end, k_end, gamma, rkk, v = ([] for _ in range(9))
    for bi in range(nb):
        lw = lw_ref[bi]
        cum = _dot3(tri, lw)
        cum_last = cum[c - 1:c, :]
        g_out = jnp.exp(-cum)
        g_end = jnp.exp(cum_last - cum)
        a_hat.append(-kkn_ref[bi] * jnp.exp(cum - lw))
        b_hat.append(b_ref[bi] * g_out)
        k_hat.append(k_ref[bi] * g_out)
        r_hat.append(r_ref[bi] * jnp.exp(cum))
        b_end.append(b_ref[bi] * g_end)
        k_end.append(k_ref[bi] * g_end)
        gamma.append(jnp.exp(cum_last))
        rkk.append(r_ref[bi] * k_ref[bi] * rk_ref[...])
        v.append(v_ref[bi])

    row2 = lax.broadcasted_iota(jnp.int32, (2 * c, 2 * c), 0)
    col2 = lax.broadcasted_iota(jnp.int32, (2 * c, 2 * c), 1)
    rt = jnp.where(row2 >= c, row2 - c, row2)
    cs = jnp.where(col2 >= c, col2 - c, col2)
    keep = cs < rt + jnp.where(row2 >= c, 1, 0)

    eye = (row == col).astype(F32)
    n_sq = int(math.log2(c))
    probs = [(bi, h) for bi in range(nb) for h in range(RW_HEADS)]
    n = range(len(probs))
    hsl = [slice(h * RW_DH, (h + 1) * RW_DH) for _, h in probs]
    bf = lambda x: x.astype(BF16)
    s_old = [s_ref[bi, h] for bi, h in probs]
    a_h = [a_hat[bi][:, hsl[i]] for i, (bi, _) in enumerate(probs)]
    v_h = [v[bi][:, hsl[i]] for i, (bi, _) in enumerate(probs)]
    r_h = [r_hat[bi][:, hsl[i]] for i, (bi, _) in enumerate(probs)]
    vb = [bf(x) for x in v_h]
    prod = [jnp.where(keep, _dot(bf(jnp.concatenate([a_h[i], r_h[i]], axis=0)),
                                 bf(jnp.concatenate([b_hat[bi][:, hsl[i]], k_hat[bi][:, hsl[i]]], axis=0)),
                                 ((1,), (1,))), 0.0) for i, (bi, _) in enumerate(probs)]
    p = [prod[i][:c, :c] for i in n]
    t = [eye + p[i] for i in n]
    w = [_dot(bf(prod[i][:c, c:]), vb[i]) for i in n]
    for _ in range(n_sq - 1):
        pb = [bf(p[i]) for i in n]
        p = [_dot(pb[i], pb[i]) for i in n]
        t = [t[i] + _dot(bf(p[i]), bf(t[i])) for i in n]
    tb = [bf(t[i]) for i in n]
    u0 = [_dot(tb[i], bf(w[i])) for i in n]
    ta = [_dot(tb[i], bf(a_h[i])) for i in n]
    xs = [_dot(bf(jnp.concatenate([ta[i], r_h[i]], axis=0)), bf(s_old[i]), ((1,), (1,))) for i in n]
    uv = [bf(jnp.concatenate([xs[i][:c] + u0[i], v_h[i]], axis=0)) for i in n]
    s_new = [s_old[i] * gamma[bi][:, hsl[i]]
             + _dot(uv[i], bf(jnp.concatenate([b_end[bi][:, hsl[i]], k_end[bi][:, hsl[i]]], axis=0)),
                    ((0,), (0,))) for i, (bi, _) in enumerate(probs)]
    y = [xs[i][c:] + _dot(bf(prod[i][c:, :]), uv[i]) for i in n]
    for i, (bi, h) in enumerate(probs):
        s_ref[bi, h] = s_new[i]
    for i, (bi, h) in enumerate(probs):
        sl = hsl[i]
        mu = jnp.mean(y[i], axis=-1, keepdims=True)
        yc = y[i] - mu
        var = jnp.mean(yc * yc, axis=-1, keepdims=True)
        yn = yc * lax.rsqrt(var + LNX_EPS) * lnw_ref[:, sl] + lnb_ref[:, sl]
        bonus = jnp.sum(rkk[bi][:, sl], axis=-1, keepdims=True) * v_h[i]
        out_ref[bi, :, sl] = (yn + bonus) * g_ref[bi, :, sl]


def _wkv(prep, s0, p):
    r = prep[0]
    bx, tx, _ = r.shape
    c = WKV_CHUNK
    nb = WKV_BATCH
    blk = pl.BlockSpec((nb, c, RW_W), lambda b, t: (b, t, 0))
    vec = pl.BlockSpec((1, RW_W), lambda b, t: (0, 0))
    st = pl.BlockSpec((nb, RW_HEADS, RW_DH, RW_DH), lambda b, t: (b, 0, 0, 0))
    return pl.pallas_call(
        _wkv_kernel,
        grid=(bx // nb, tx // c),
        in_specs=[blk] * 7 + [st, vec, vec, vec],
        out_specs=[blk, st],
        out_shape=[jax.ShapeDtypeStruct((bx, tx, RW_W), F32),
                   jax.ShapeDtypeStruct((bx, RW_HEADS, RW_DH, RW_DH), F32)],
        compiler_params=_cparams(("parallel", "arbitrary")),
        name="wkv_chunk",
    )(*prep, s0, p["r_k"], p["lnx_w"], p["lnx_b"])


def _attn_kernel(lam_init, lam_ref, q_ref, k_ref, v_ref, sw_ref, o_ref, kt_ref, vx_ref, m_ref, acc_ref):
    qi = pl.program_id(2)
    tq = q_ref.shape[0]

    @pl.when(qi == 0)
    def _():
        kt_ref[...] = k_ref[...].T.astype(BF16)
        vx_ref[:, :DF_DV] = v_ref[...].astype(BF16)
        vx_ref[:, DF_DV:] = jnp.ones((vx_ref.shape[0], DF_DV), BF16)

    q = (q_ref[...] * (DF_DK ** -0.5)).astype(BF16)
    qc = [q[:, :DF_DK], q[:, DF_DK:]]
    m_ref[...] = jnp.full(m_ref.shape, -jnp.inf, F32)
    acc_ref[...] = jnp.zeros(acc_ref.shape, F32)
    def block(start, width, masked):
        vblk = vx_ref[pl.ds(start, width), :]
        s = [_dot(qc[c], kt_ref[c * DF_DK:(c + 1) * DF_DK, pl.ds(start, width)]) for c in range(2)]
        if masked:
            rowi = lax.broadcasted_iota(jnp.int32, (tq, width), 0)
            coli = lax.broadcasted_iota(jnp.int32, (tq, width), 1)
            causal = coli <= rowi + (width - tq)
            s = [jnp.where(causal, s[c], -jnp.inf) for c in range(2)]
        m_old = [m_ref[c] for c in range(2)]
        m_new = [jnp.maximum(m_old[c], jnp.max(s[c], axis=-1, keepdims=True)) for c in range(2)]
        pr = [jnp.exp(s[c] - m_new[c]).astype(BF16) for c in range(2)]
        pv = [_dot(pr[c], vblk) for c in range(2)]
        for c in range(2):
            acc_ref[c] = jnp.exp(m_old[c] - m_new[c]) * acc_ref[c] + pv[c]
            m_ref[c] = m_new[c]

    def pair(pj, carry):
        block(pl.multiple_of(pj * (2 * tq), 2 * tq), 2 * tq, False)
        return carry

    lax.fori_loop(0, qi // 2, pair, 0)

    @pl.when(qi % 2 == 1)
    def _():
        block(pl.multiple_of((qi - 1) * tq, tq), 2 * tq, True)

    @pl.when(qi % 2 == 0)
    def _():
        block(pl.multiple_of(qi * tq, tq), tq, True)
    o = (acc_ref[0, :, :DF_DV] / acc_ref[0, :, DF_DV:]
         - lam_ref[0] * (acc_ref[1, :, :DF_DV] / acc_ref[1, :, DF_DV:]))
    o = o * lax.rsqrt(jnp.mean(o * o, axis=-1, keepdims=True) + SUBLN_EPS)
    o_ref[...] = o * sw_ref[...] * (1.0 - lam_init)


def _attn_prompt(q, k, v, b, tp, lam, subln_w, lam_init):
    tq = ATT_TILE
    nq = tp // tq
    return pl.pallas_call(
        functools.partial(_attn_kernel, lam_init),
        grid=(b, DF_HEADS, nq),
        in_specs=[pl.BlockSpec(memory_space=pltpu.SMEM),
                  pl.BlockSpec((tq, DF_DV), lambda bi, h, i: (bi * nq + i, h)),
                  pl.BlockSpec((tp, DF_DV), lambda bi, h, i: (bi, h)),
                  pl.BlockSpec((tp, DF_DV), lambda bi, h, i: (bi, h)),
                  pl.BlockSpec((1, DF_DV), lambda bi, h, i: (0, h))],
        out_specs=pl.BlockSpec((tq, DF_DV), lambda bi, h, i: (bi * nq + i, h)),
        out_shape=jax.ShapeDtypeStruct((b * tp, DF_W), F32),
        scratch_shapes=[pltpu.VMEM((2 * DF_DK, tp), BF16), pltpu.VMEM((tp, 2 * DF_DV), BF16),
                        pltpu.VMEM((2, tq, 1), F32), pltpu.VMEM((2, tq, 2 * DF_DV), F32)],
        compiler_params=_cparams(("parallel", "parallel", "arbitrary")),
        name="diff_attn_prompt",
    )(lam, q, k, v, subln_w)


def _decode_kernel(lam_init, n_pg, pt_ref, lam_ref, q_ref, kn_ref, vn_ref, *refs):
    ck_refs = refs[:n_pg]
    cv_refs = refs[n_pg:2 * n_pg]
    sw_ref, o_ref, qrow_ref, m_ref, l_ref, acc_ref = refs[2 * n_pg:]
    pi = pl.program_id(1)
    n_pages = pl.num_programs(1)
    nq = 2 * DF_HEADS
    page = ck_refs[0].shape[2]

    @pl.when(pi == 0)
    def _():
        rowi = lax.broadcasted_iota(jnp.int32, (nq, DF_W), 0)
        coli = lax.broadcasted_iota(jnp.int32, (nq, DF_W), 1)
        qb = jnp.broadcast_to(q_ref[0] * (DF_DK ** -0.5), (nq, DF_W))
        qrow_ref[...] = jnp.where(jnp.right_shift(coli, int(math.log2(DF_DK))) == rowi, qb, 0.0)
        m_ref[...] = jnp.full(m_ref.shape, -jnp.inf, F32)
        l_ref[...] = jnp.zeros(l_ref.shape, F32)
        acc_ref[...] = jnp.zeros(acc_ref.shape, F32)

    qrow = qrow_ref[...]
    qb = qrow.astype(BF16)
    s = jnp.concatenate([_dot(qb, ck_refs[g][0].astype(BF16)) for g in range(n_pg)], axis=1)
    m_old = m_ref[...]
    m_new = jnp.maximum(m_old, jnp.max(s, axis=-1, keepdims=True))
    alpha = jnp.exp(m_old - m_new)
    pr = jnp.exp(s - m_new)
    l_ref[...] = alpha * l_ref[...] + jnp.sum(pr, axis=-1, keepdims=True)
    prb = pr.astype(BF16)
    for h in range(DF_HEADS):
        pv = None
        for g in range(n_pg):
            v_gh = cv_refs[g][0, pl.ds(h, page, stride=DF_HEADS), :].astype(BF16)
            d = _dot(prb[:, g * page:(g + 1) * page], v_gh)
            pv = d if pv is None else pv + d
        sl = slice(h * DF_DV, (h + 1) * DF_DV)
        acc_ref[:, sl] = alpha * acc_ref[:, sl] + pv
    m_ref[...] = m_new

    @pl.when(pi == n_pages - 1)
    def _():
        s_self = jnp.sum(qrow * kn_ref[0], axis=-1, keepdims=True)
        m_old = m_ref[...]
        m_fin = jnp.maximum(m_old, s_self)
        alpha = jnp.exp(m_old - m_fin)
        p_self = jnp.exp(s_self - m_fin)
        l_fin = alpha * l_ref[...] + p_self
        acc = (alpha * acc_ref[...] + p_self * vn_ref[0]) / l_fin
        for h in range(DF_HEADS):
            sl = slice(h * DF_DV, (h + 1) * DF_DV)
            o = acc[2 * h:2 * h + 1, sl] - lam_ref[0] * acc[2 * h + 1:2 * h + 2, sl]
            o = o * lax.rsqrt(jnp.mean(o * o, axis=-1, keepdims=True) + SUBLN_EPS)
            o_ref[0, :, sl] = o * sw_ref[:, sl] * (1.0 - lam_init)


def _attn_sample(q, k_new, v_new, cache_k, cache_v, page_table, lam, subln_w, lam_init):
    bs = q.shape[0]
    n_pages = page_table.shape[1]
    n_pool, page = cache_k.shape[:2]
    ck = jnp.transpose(cache_k, (0, 2, 3, 4, 1)).reshape(n_pool, DF_W, page)
    cv = cache_v.reshape(n_pool, page * DF_HEADS, DF_DV)
    n_pg = math.gcd(DEC_PAGES, n_pages)
    row = pl.BlockSpec((1, 1, DF_W), lambda b, p, pt: (b, 0, 0))
    ck_specs = [pl.BlockSpec((1, DF_W, page), lambda b, p, pt, g=g: (pt[b, p * n_pg + g], 0, 0))
                for g in range(n_pg)]
    cv_specs = [pl.BlockSpec((1, page * DF_HEADS, DF_DV), lambda b, p, pt, g=g: (pt[b, p * n_pg + g], 0, 0))
                for g in range(n_pg)]
    nq = 2 * DF_HEADS
    grid_spec = pltpu.PrefetchScalarGridSpec(
        num_scalar_prefetch=1,
        grid=(bs, n_pages // n_pg),
        in_specs=[pl.BlockSpec(memory_space=pltpu.SMEM), row, row, row] + ck_specs + cv_specs
                 + [pl.BlockSpec((1, DF_W), lambda b, p, pt: (0, 0))],
        out_specs=row,
        scratch_shapes=[pltpu.VMEM((nq, DF_W), F32), pltpu.VMEM((nq, 1), F32),
                        pltpu.VMEM((nq, 1), F32), pltpu.VMEM((nq, DF_W), F32)],
    )
    out = pl.pallas_call(
        functools.partial(_decode_kernel, lam_init, n_pg),
        grid_spec=grid_spec,
        out_shape=jax.ShapeDtypeStruct((bs, 1, DF_W), F32),
        compiler_params=_cparams(("parallel", "arbitrary")),
        name="diff_attn_paged",
    )(page_table, lam, q.reshape(bs, 1, DF_W), k_new.reshape(bs, 1, DF_W), v_new.reshape(bs, 1, DF_W),
      *([ck] * n_pg), *([cv] * n_pg), subln_w)
    return out.reshape(bs, DF_W)


def _layer_norm(x, w, b):
    mu = jnp.mean(x, axis=-1, keepdims=True)
    xc = x - mu
    var = jnp.mean(xc * xc, axis=-1, keepdims=True)
    return xc * lax.rsqrt(var + LN_EPS) * w + b


def _outproj_kernel(n_main, x_ref, rw_ref, o_ref, rwt_ref, ot_ref, w_ref, lnw_ref, lnb_ref, y_ref, yt_ref):
    main = pl.program_id(0) < n_main
    rw = jnp.where(main, rw_ref[...], rwt_ref[...]).astype(BF16)
    o = jnp.where(main, o_ref[...], ot_ref[...]).astype(BF16)
    mix = _dot(rw, w_ref[:RW_W, :]) + _dot(o, w_ref[RW_W:, :])
    y = _layer_norm(DEEP_ALPHA * x_ref[...] + mix, lnw_ref[...], lnb_ref[...])
    y_ref[...] = y
    yt_ref[...] = y.T.astype(BF16)


def _outproj(x, rw_main, o_main, rw_tail, o_tail, w_bf, ln_w, ln_b):
    rows, d = x.shape
    tm = ROW_TILE
    n_main = rw_main.shape[0] // tm
    assert rw_main.shape[0] % tm == 0 and rows == (n_main + 1) * tm and rw_tail.shape[0] == tm
    row = lambda i: (i, 0)
    main = lambda i: (jnp.minimum(i, n_main - 1), 0)
    tail = lambda i: (0, 0)
    return pl.pallas_call(
        functools.partial(_outproj_kernel, n_main),
        grid=(rows // tm,),
        in_specs=[pl.BlockSpec((tm, d), row), pl.BlockSpec((tm, RW_W), main), pl.BlockSpec((tm, DF_W), main),
                  pl.BlockSpec((tm, RW_W), tail), pl.BlockSpec((tm, DF_W), tail),
                  pl.BlockSpec(w_bf.shape, lambda i: (0, 0)),
                  pl.BlockSpec((1, d), lambda i: (0, 0)), pl.BlockSpec((1, d), lambda i: (0, 0))],
        out_specs=[pl.BlockSpec((tm, d), row), pl.BlockSpec((d, tm), lambda i: (0, i))],
        out_shape=[jax.ShapeDtypeStruct((rows, d), F32), jax.ShapeDtypeStruct((d, rows), BF16)],
        compiler_params=_cparams(("parallel",)),
        name="outproj_ln",
    )(x, rw_main, o_main, rw_tail, o_tail, w_bf, ln_w, ln_b)


def _peer_pairs():
    n = PK_TOPK + 1
    return [(i, j) for i in range(n) for j in range(n) if (i + 1) * (j + 1) <= n]


def _sort_pairs(n):
    pairs = []
    p = 1
    while p < n:
        k = p
        while k >= 1:
            for j in range(k % p, n - k, 2 * k):
                for i in range(min(k, n - j - k)):
                    if (i + j) // (2 * p) == (i + j + k) // (2 * p):
                        pairs.append((i + j, i + j + k))
            k //= 2
        p *= 2
    return pairs


def _extract_top(cur, n, n_pad):
    rows, cols = cur.shape
    n_tiles = rows // SUBLANES
    size = 1 << (n_tiles - 1).bit_length()
    neg = jnp.full((SUBLANES, cols), -jnp.inf, F32)
    t = [cur[i * SUBLANES:(i + 1) * SUBLANES, :] for i in range(n_tiles)] + [neg] * (size - n_tiles)
    for i, j in _sort_pairs(size):
        t[i], t[j] = jnp.maximum(t[i], t[j]), jnp.minimum(t[i], t[j])
    vals = []
    rowi = lax.broadcasted_iota(jnp.int32, (n_pad, cols), 0)
    stacked = jnp.zeros((n_pad, cols), F32)
    for k in range(n):
        m = jnp.max(t[0], axis=0, keepdims=True)
        vals.append(m)
        stacked = jnp.where(rowi == k, m, stacked)
        hit = t[0] == m
        keep = min(len(t), n - 1 - k)
        t = [jnp.where(hit, t[l + 1] if l + 1 < len(t) else neg, t[l]) for l in range(keep)]
    return vals, stacked


def _peer_score_kernel(x_ref, wq_ref, sk_ref, sela_ref, selb_ref,
                       e1_ref, e2_ref, psi_ref, qt_ref):
    n_top = PK_TOPK + 1
    half = PK_DQ // 2
    tt = x_ref.shape[0]
    x = x_ref[...]
    x_hi = x.astype(BF16)
    x_lo = (x - x_hi.astype(F32)).astype(BF16)
    nt = ((1,), (1,))
    qt_ref[...] = (_dot(wq_ref[0], x_hi, nt)
                   + (_dot(wq_ref[1], x_hi, nt) + _dot(wq_ref[0], x_lo, nt)))
    n_cand = sela_ref.shape[0]
    n_pad = sela_ref.shape[1]
    cand_row = lax.broadcasted_iota(jnp.int32, (n_cand, tt), 0)
    n_pairs = len(_peer_pairs())
    for h in range(PK_HEADS):
        s1 = _dot3(sk_ref[h, 0], qt_ref[pl.ds(h * PK_DQ, half), :])
        s2 = _dot3(sk_ref[h, 1], qt_ref[pl.ds(h * PK_DQ + half, half), :])
        top1, a_st = _extract_top(s1, n_top, n_pad)
        top2, b_st = _extract_top(s2, n_top, n_pad)
        cand = _dot(sela_ref[...], a_st, precision=HI) + _dot(selb_ref[...], b_st, precision=HI)
        cand = jnp.where(cand_row < n_pairs, cand, -jnp.inf)
        ctop, _ = _extract_top(cand, n_top, n_pad)
        cmax = ctop[0]
        theta = 0.5 * (ctop[PK_TOPK - 1] + ctop[PK_TOPK])
        z = jnp.sum(jnp.where(cand >= theta, jnp.exp(cand - cmax), 0.0), axis=0, keepdims=True)
        e1_ref[h] = jnp.exp(s1 - top1[0]) * (0.5 / z)
        e2_ref[h] = jnp.exp(s2 - top2[0])
        psi_ref[h] = jnp.exp(theta - s1 - top2[0])


def _peer_scores(x1, wq_t, subkeys, sel_a, sel_b):
    rows, d = x1.shape
    tt = PEER_TOK
    out = jax.ShapeDtypeStruct((PK_HEADS, N_KEYS, rows), F32)
    ospec = pl.BlockSpec((PK_HEADS, N_KEYS, tt), lambda i: (0, 0, i))
    return pl.pallas_call(
        _peer_score_kernel,
        grid=(rows // tt,),
        in_specs=[pl.BlockSpec((tt, d), lambda i: (i, 0)),
                  pl.BlockSpec(wq_t.shape, lambda i: (0, 0, 0)),
                  pl.BlockSpec(subkeys.shape, lambda i: (0, 0, 0, 0)),
                  pl.BlockSpec(sel_a.shape, lambda i: (0, 0)),
                  pl.BlockSpec(sel_b.shape, lambda i: (0, 0))],
        out_specs=[ospec] * 3,
        out_shape=[out] * 3,
        scratch_shapes=[pltpu.VMEM((PK_HEADS * PK_DQ, tt), F32)],
        compiler_params=_cparams(("parallel",)),
        name="peer_scores",
    )(x1, wq_t, subkeys, sel_a, sel_b)


def _peer_expert_kernel(x_ref, xt_ref, u_ref, vt_ref, e1_ref, e2_ref, psi_ref, lnw_ref, lnb_ref,
                        y_ref, wt_ref, acc_ref, act_ref, pb_ref, eb_ref):
    ei = pl.program_id(1)
    te = u_ref.shape[0]
    tt = x_ref.shape[0]

    @pl.when(ei == 0)
    def _():
        acc_ref[...] = jnp.zeros(acc_ref.shape, F32)

    n_rows = te // N_KEYS
    k_rows = PEER_KSPLIT // N_KEYS
    n_sub = N_KEYS // SUBLANES
    for j in range(n_rows):
        i1 = ei * n_rows + j
        for h in range(PK_HEADS):
            pb_ref[j, h] = jnp.broadcast_to(psi_ref[h, pl.ds(i1, 1), :], (SUBLANES, tt))
            eb_ref[j, h] = jnp.broadcast_to(e1_ref[h, pl.ds(i1, 1), :], (SUBLANES, tt))
    n_grp = n_rows // k_rows

    def first_product(g):
        act_ref[g % 2] = _dot(u_ref[g * PEER_KSPLIT:(g + 1) * PEER_KSPLIT, :], xt_ref[...])

    def second_product(g):
        ks = slice(g * PEER_KSPLIT, (g + 1) * PEER_KSPLIT)
        return _dot(vt_ref[:, ks], wt_ref[ks, :])

    def gate_row(j):
        rows = slice(j * N_KEYS, (j + 1) * N_KEYS)
        arow = slice((j % k_rows) * N_KEYS, (j % k_rows + 1) * N_KEYS)
        for tg in range(tt // LANES):
            ls = slice(tg * LANES, (tg + 1) * LANES)
            gate = jnp.zeros((n_sub, SUBLANES, LANES), F32)
            for h in range(PK_HEADS):
                e2 = e2_ref[h, :, ls].reshape(n_sub, SUBLANES, LANES)
                gate = gate + jnp.where(e2 >= pb_ref[j, h, :, ls][None], eb_ref[j, h, :, ls][None] * e2, 0.0)
            a = act_ref[(j // k_rows) % 2, arow, ls]
            gate = gate.reshape(N_KEYS, LANES)
            wt_ref[rows, ls] = (gate * (a + a * lax.erf(a * (2.0 ** -0.5)))).astype(BF16)

    first_product(0)
    part = None
    for g in range(n_grp):
        if g + 1 < n_grp:
            first_product(g + 1)
        for r in range(k_rows):
            gate_row(g * k_rows + r)
            if r == k_rows // 2 - 1 and g > 0:
                d = second_product(g - 1)
                part = d if part is None else part + d
    d = second_product(n_grp - 1)
    acc_ref[...] += d if part is None else part + d

    @pl.when(ei == pl.num_programs(1) - 1)
    def _():
        ch = acc_ref[...].T
        y_ref[...] = _layer_norm(DEEP_ALPHA * x_ref[...] + ch, lnw_ref[...], lnb_ref[...])


def _peer_experts(x1, x1_t, u_bf, vt_bf, e1, e2, psi, ln_w, ln_b):
    rows, d = x1.shape
    n_exp = u_bf.shape[0]
    tt = PEER_TOK
    te = PEER_EXP
    tok = lambda i, e: (i, 0)
    sc = pl.BlockSpec((PK_HEADS, N_KEYS, tt), lambda i, e: (0, 0, i))
    vec = pl.BlockSpec((1, d), lambda i, e: (0, 0))
    return pl.pallas_call(
        _peer_expert_kernel,
        grid=(rows // tt, n_exp // te),
        in_specs=[pl.BlockSpec((tt, d), tok), pl.BlockSpec((d, tt), lambda i, e: (0, i)),
                  pl.BlockSpec((te, d), lambda i, e: (e, 0)),
                  pl.BlockSpec((d, te), lambda i, e: (0, e)),
                  sc, sc, sc, vec, vec],
        out_specs=pl.BlockSpec((tt, d), tok),
        out_shape=jax.ShapeDtypeStruct((rows, d), F32),
        scratch_shapes=[pltpu.VMEM((te, tt), BF16), pltpu.VMEM((d, tt), F32),
                        pltpu.VMEM((2, PEER_KSPLIT, tt), F32),
                        pltpu.VMEM((te // N_KEYS, PK_HEADS, SUBLANES, tt), F32),
                        pltpu.VMEM((te // N_KEYS, PK_HEADS, SUBLANES, tt), F32)],
        compiler_params=_cparams(("parallel", "arbitrary")),
        name="peer_experts",
    )(x1, x1_t, u_bf, vt_bf, e1, e2, psi, ln_w, ln_b)


def _rope_tables(pos):
    half = DF_DK // 2
    inv = ROPE_THETA ** (-jnp.arange(half, dtype=F32) / half)
    ang = pos.astype(F32)[:, None] * inv[None, :]
    cos, sin = jnp.cos(ang), jnp.sin(ang)
    cos128 = jnp.concatenate([cos, cos, cos, cos], axis=1)
    sin128 = jnp.concatenate([-sin, sin, -sin, sin], axis=1)
    return cos128, sin128


def _round_up(n, m):
    return -(-n // m) * m


def kernel(x_prompt, x_sample, cache_k, cache_v, page_table, state_wkv, state_shift, meta_tokens, w_in,
           mu_shift, w0, w2_decay, a0, a2, g2, k_k, k_a, r_k, lnx_w, lnx_b, lam_q1, lam_k1, lam_q2, lam_k2,
           subln_w, w_out, ln1_w, ln1_b, pk_wq, pk_subkeys, peer_u, peer_v, ln2_w, ln2_b):
    assert w_in.shape[0] == DEPTH == 1
    bp, seq, d = x_prompt.shape
    bs, dec_seq, _ = x_sample.shape
    assert dec_seq == 1
    t_prompt = N_META + seq
    tp = _round_up(t_prompt, math.lcm(ATT_TILE, PREP_TILE, WKV_CHUNK))
    n_prompt_rows = bp * tp
    rows = _round_up(n_prompt_rows + bs, ROW_TILE)
    page = cache_k.shape[2]
    past = page_table.shape[1] * page
    lam_init = 0.8 - 0.6 * math.exp(-0.3 * 0)
    row2 = lambda a: a.reshape(1, -1)

    meta = meta_tokens.astype(x_prompt.dtype)
    pad = jnp.zeros((tp - t_prompt, d), x_prompt.dtype)
    pieces = [p for b in range(bp) for p in (meta, x_prompt[b], pad)]
    x_all = jnp.concatenate(pieces + [x_sample.reshape(bs, d),
                                      jnp.zeros((rows - n_prompt_rows - bs, d), x_prompt.dtype)], axis=0)
    pos = jnp.concatenate([jnp.tile(jnp.arange(tp), bp), jnp.full((rows - n_prompt_rows,), past)])
    cos, sin = _rope_tables(pos)

    rw_cols, q, k, v = _inproj(x_all, w_in[0].astype(BF16), cos, sin)

    sm = lambda a: a[n_prompt_rows:n_prompt_rows + bs]

    hs = jnp.arange(RW_W) // RW_DH
    rw_params = {
        "mu_shift": row2(mu_shift[0]), "w0": row2(w0[0]), "w2_decay": w2_decay[0], "a0": row2(a0[0]),
        "a2": a2[0], "g2": g2[0], "k_k": row2(k_k[0]), "k_a": row2(k_a[0]),
        "headsum": (hs[:, None] == hs[None, :]).astype(F32),
        "r_k": row2(r_k[0]), "lnx_w": row2(lnx_w[0]), "lnx_b": row2(lnx_b[0]),
    }
    prep_p = _rwkv_prep(rw_cols, bp, tp, jnp.zeros((bp, RW_COLS), F32), t_prompt, rw_params)
    rw_out_p, wkv_p = _wkv(prep_p, jnp.zeros((bp, RW_HEADS, RW_DH, RW_DH), F32), rw_params)
    ts = WKV_CHUNK
    rw_s = jnp.concatenate([sm(rw_cols)[:, None, :], jnp.zeros((bs, ts - 1, RW_COLS), F32)], axis=1)
    prep_s = _rwkv_prep(rw_s.reshape(bs * ts, RW_COLS), bs, ts, state_shift[0], dec_seq, rw_params)
    rw_out_s, wkv_s = _wkv(prep_s, state_wkv[0], rw_params)

    f32 = lambda t: t.astype(F32)
    lam = (jnp.exp(jnp.sum(f32(lam_q1[0]) * f32(lam_k1[0])))
           - jnp.exp(jnp.sum(f32(lam_q2[0]) * f32(lam_k2[0]))) + lam_init).reshape(1)
    sw = row2(subln_w[0])
    o_p = _attn_prompt(q, k, v, bp, tp, lam, sw, lam_init)
    o_s = _attn_sample(sm(q), sm(k), sm(v), cache_k[0], cache_v[0], page_table, lam, sw, lam_init)

    tail = jnp.zeros((rows - n_prompt_rows - bs, RW_W), F32)
    x1, x1_t = _outproj(x_all, rw_out_p.reshape(n_prompt_rows, RW_W), o_p,
                        jnp.concatenate([rw_out_s[:, 0, :], tail], axis=0), jnp.concatenate([o_s, tail], axis=0),
                        w_out[0].astype(BF16), row2(ln1_w[0]), row2(ln1_b[0]))

    pairs = _peer_pairs()
    n_cand = _round_up(len(pairs), SUBLANES)
    n_pad = _round_up(PK_TOPK + 1, SUBLANES)
    ia = jnp.array([p[0] for p in pairs] + [0] * (n_cand - len(pairs)))
    ib = jnp.array([p[1] for p in pairs] + [0] * (n_cand - len(pairs)))
    sel_a = (ia[:, None] == jnp.arange(n_pad)[None, :]).astype(F32)
    sel_b = (ib[:, None] == jnp.arange(n_pad)[None, :]).astype(F32)
    wq_t = pk_wq[0].T.astype(F32)
    wq_hi = wq_t.astype(BF16)
    wq_lo = (wq_t - wq_hi.astype(F32)).astype(BF16)
    e1, e2, psi = _peer_scores(x1, jnp.stack([wq_hi, wq_lo]), pk_subkeys[0], sel_a, sel_b)
    y = _peer_experts(x1, x1_t, peer_u[0].astype(BF16), peer_v[0].T.astype(BF16),
                      e1, e2, psi, row2(ln2_w[0]), row2(ln2_b[0]))

    seq = lambda a, lo, hi: jnp.stack([a[b * tp + lo:b * tp + hi] for b in range(bp)])
    y_prompt = seq(y, N_META, t_prompt)
    y_sample = sm(y).reshape(bs, dec_seq, d)
    k_p = seq(k, 0, t_prompt).reshape(1, bp, t_prompt, DF_HEADS, 2, DF_DK)
    v_p = seq(v, 0, t_prompt).reshape(1, bp, t_prompt, DF_HEADS, DF_DV)
    shift_p = seq(rw_cols, t_prompt - 1, t_prompt).reshape(1, bp, RW_COLS)
    k_s = sm(k).reshape(1, bs, dec_seq, DF_HEADS, 2, DF_DK)
    v_s = sm(v).reshape(1, bs, dec_seq, DF_HEADS, DF_DV)
    shift_s = sm(rw_cols)[None]
    return (y_prompt, y_sample, k_p, v_p, wkv_p[None], shift_p, k_s, v_s, wkv_s[None], shift_s)
```

```python
import functools
import math

import jax
import jax.numpy as jnp
from jax import lax
from jax.experimental import pallas as pl
from jax.experimental.pallas import tpu as pltpu

F32 = jnp.float32
BF16 = jnp.bfloat16

N_META = 16
RW_HEADS = 8
RW_DH = 64
RW_W = RW_HEADS * RW_DH
DECAY_LORA = 64
AAA_LORA = 64
GATE_LORA = 128
RW_COLS = 3 * RW_W + DECAY_LORA + AAA_LORA + GATE_LORA
LNX_EPS = 64e-5
DF_HEADS = 4
DF_DK = 64
DF_DV = 2 * DF_DK
DF_W = DF_HEADS * DF_DV
ROPE_THETA = 10000.0
SUBLN_EPS = 1e-5
LN_EPS = 1e-5
PK_HEADS = 8
N_KEYS = 128
PK_DQ = 128
PK_TOPK = 16
DEPTH = 1
DEEP_ALPHA = (2 * DEPTH) ** 0.25

LANES = 128
SUBLANES = 8
VMEM_LIMIT = 56 * 1024 * 1024

ROW_TILE = 512
PREP_TILE = 128
WKV_CHUNK = 64
WKV_BATCH = 4
ATT_TILE = 384
DEC_PAGES = 32
PEER_TOK = 512
PEER_EXP = 2048
PEER_KSPLIT = 512
HI = lax.Precision.HIGHEST


def _cparams(sem):
    return pltpu.CompilerParams(dimension_semantics=sem, vmem_limit_bytes=VMEM_LIMIT)


def _dot(a, b, dims=((1,), (0,)), precision=None):
    return lax.dot_general(a, b, (dims, ((), ())), precision=precision,
                           preferred_element_type=F32)


def _dot3(a, b, dims=((1,), (0,))):
    a_hi = a.astype(BF16)
    a_lo = (a - a_hi.astype(F32)).astype(BF16)
    b_hi = b.astype(BF16)
    b_lo = (b - b_hi.astype(F32)).astype(BF16)
    return _dot(a_hi, b_hi, dims) + (_dot(a_lo, b_hi, dims) + _dot(a_hi, b_lo, dims))


def _rope128(z, cos, sin_signed, first_half):
    partner = jnp.where(first_half, pltpu.roll(z, LANES - 32, 1), pltpu.roll(z, 32, 1))
    return z * cos + partner * sin_signed


def _inproj_kernel(x_ref, w_ref, cos_ref, sin_ref, rw_ref, q_ref, k_ref, v_ref):
    x = x_ref[...].astype(BF16)
    rw_ref[...] = _dot(x, w_ref[:, :RW_COLS])
    cos = cos_ref[...]
    sin = sin_ref[...]
    lane = lax.broadcasted_iota(jnp.int32, cos.shape, 1)
    first_half = (lane & 32) == 0
    q = _dot(x, w_ref[:, RW_COLS:RW_COLS + DF_W])
    k = _dot(x, w_ref[:, RW_COLS + DF_W:RW_COLS + 2 * DF_W])
    for g in range(DF_W // LANES):
        sl = slice(g * LANES, (g + 1) * LANES)
        q_ref[:, sl] = _rope128(q[:, sl], cos, sin, first_half)
        k_ref[:, sl] = _rope128(k[:, sl], cos, sin, first_half)
    v_ref[...] = _dot(x, w_ref[:, RW_COLS + 2 * DF_W:])


def _inproj(x, w_bf, cos, sin):
    rows, d = x.shape
    in_cols = w_bf.shape[1]
    tm = ROW_TILE
    row = lambda i: (i, 0)
    return pl.pallas_call(
        _inproj_kernel,
        grid=(rows // tm,),
        in_specs=[pl.BlockSpec((tm, d), row),
                  pl.BlockSpec((d, in_cols), lambda i: (0, 0)),
                  pl.BlockSpec((tm, LANES), row),
                  pl.BlockSpec((tm, LANES), row)],
        out_specs=[pl.BlockSpec((tm, RW_COLS), row),
                   pl.BlockSpec((tm, DF_W), row),
                   pl.BlockSpec((tm, DF_W), row),
                   pl.BlockSpec((tm, DF_W), row)],
        out_shape=[jax.ShapeDtypeStruct((rows, RW_COLS), F32),
                   jax.ShapeDtypeStruct((rows, DF_W), F32),
                   jax.ShapeDtypeStruct((rows, DF_W), F32),
                   jax.ShapeDtypeStruct((rows, DF_W), F32)],
        compiler_params=_cparams(("parallel",)),
        name="inproj_rope",
    )(x, w_bf, cos, sin)


def _prep_kernel(t_real, cols_ref, shift0_ref, mu_ref, w0_ref, w2_ref, a0_ref, a2_ref, g2_ref,
                 kk_ref, ka_ref, headsum_ref,
                 r_ref, lw_ref, k_ref, v_ref, kkn_ref, b_ref, g_ref, carry_ref):
    ti = pl.program_id(1)
    tm = cols_ref.shape[0]

    @pl.when(ti == 0)
    def _():
        carry_ref[...] = jnp.broadcast_to(shift0_ref[0], carry_ref.shape)

    x = cols_ref[...]
    rowi = lax.broadcasted_iota(jnp.int32, x.shape, 0)
    prev = jnp.where(rowi == 0, carry_ref[0:1, :], pltpu.roll(x, 1, 0))
    carry_ref[...] = jnp.broadcast_to(x[tm - 1:tm, :], carry_ref.shape)
    mixed = x + (prev - x) * mu_ref[...]
    r = mixed[:, 0:RW_W]
    k = mixed[:, RW_W:2 * RW_W]
    v = mixed[:, 2 * RW_W:3 * RW_W]
    o = 3 * RW_W
    xw = mixed[:, o:o + DECAY_LORA]
    xa = mixed[:, o + DECAY_LORA:o + DECAY_LORA + AAA_LORA]
    xg = mixed[:, o + DECAY_LORA + AAA_LORA:]
    d = w0_ref[...] + _dot3(jnp.tanh(xw), w2_ref[...])
    lw = -jax.nn.sigmoid(d) * math.exp(-0.5)
    a = jax.nn.sigmoid(a0_ref[...] + _dot3(xa, a2_ref[...]))
    g = _dot3(jax.nn.sigmoid(xg), g2_ref[...])
    kk = k * kk_ref[...]
    k2 = k * (1.0 + (a - 1.0) * ka_ref[...])
    ssq = _dot3(kk * kk, headsum_ref[...])
    kkn = kk / jnp.maximum(jnp.sqrt(ssq), 1e-12)
    valid = (ti * tm + lax.broadcasted_iota(jnp.int32, (tm, RW_W), 0)) < t_real
    r_ref[0] = r
    lw_ref[0] = jnp.where(valid, lw, 0.0)
    k_ref[0] = jnp.where(valid, k2, 0.0)
    v_ref[0] = v
    kkn_ref[0] = jnp.where(valid, kkn, 0.0)
    b_ref[0] = jnp.where(valid, kkn * a, 0.0)
    g_ref[0] = g


def _rwkv_prep(cols, bx, tx, shift0, t_real, p):
    tm = min(PREP_TILE, tx)
    blk = lambda b, t: (b, t, 0)
    vec = lambda n: pl.BlockSpec((1, n), lambda b, t: (0, 0))
    mat = lambda r, c: pl.BlockSpec((r, c), lambda b, t: (0, 0))
    outs = [jax.ShapeDtypeStruct((bx, tx, RW_W), F32)] * 7
    return pl.pallas_call(
        functools.partial(_prep_kernel, t_real),
        grid=(bx, tx // tm),
        in_specs=[pl.BlockSpec((tm, RW_COLS), lambda b, t: (b * (tx // tm) + t, 0)),
                  pl.BlockSpec((1, 1, RW_COLS), lambda b, t: (b, 0, 0)),
                  vec(RW_COLS), vec(RW_W), mat(DECAY_LORA, RW_W), vec(RW_W), mat(AAA_LORA, RW_W),
                  mat(GATE_LORA, RW_W), vec(RW_W), vec(RW_W), mat(RW_W, RW_W)],
        out_specs=[pl.BlockSpec((1, tm, RW_W), blk)] * 7,
        out_shape=outs,
        scratch_shapes=[pltpu.VMEM((SUBLANES, RW_COLS), F32)],
        compiler_params=_cparams(("parallel", "arbitrary")),
        name="rwkv_prep",
    )(cols, shift0.reshape(bx, 1, RW_COLS), p["mu_shift"], p["w0"], p["w2_decay"], p["a0"], p["a2"],
      p["g2"], p["k_k"], p["k_a"], p["headsum"])


def _wkv_kernel(r_ref, lw_ref, k_ref, v_ref, kkn_ref, b_ref, g_ref, s0_ref, rk_ref, lnw_ref, lnb_ref,
                out_ref, s_ref):
    ci = pl.program_id(1)
    c = r_ref.shape[1]

    @pl.when(ci == 0)
    def _():
        s_ref[...] = s0_ref[...]

    row = lax.broadcasted_iota(jnp.int32, (c, c), 0)
    col = lax.broadcasted_iota(jnp.int32, (c, c), 1)
    tri = (col <= row).astype(F32)
    nb = r_ref.shape[0]
    a_hat, b_hat, k_hat, r_hat, b_end, k_end, gamma, rkk, v = ([] for _ in range(9))
    for bi in range(nb):
        lw = lw_ref[bi]
        cum = _dot3(tri, lw)
        cum_last = cum[c - 1:c, :]
        g_out = jnp.exp(-cum)
        g_end = jnp.exp(cum_last - cum)
        a_hat.append(-kkn_ref[bi] * jnp.exp(cum - lw))
        b_hat.append(b_ref[bi] * g_out)
        k_hat.append(k_ref[bi] * g_out)
        r_hat.append(r_ref[bi] * jnp.exp(cum))
        b_end.append(b_ref[bi] * g_end)
        k_end.append(k_ref[bi] * g_end)
        gamma.append(jnp.exp(cum_last))
        rkk.append(r_ref[bi] * k_ref[bi] * rk_ref[...])
        v.append(v_ref[bi])

    row2 = lax.broadcasted_iota(jnp.int32, (2 * c, 2 * c), 0)
    col2 = lax.broadcasted_iota(jnp.int32, (2 * c, 2 * c), 1)
    rt = jnp.where(row2 >= c, row2 - c, row2)
    cs = jnp.where(col2 >= c, col2 - c, col2)
    keep = cs < rt + jnp.where(row2 >= c, 1, 0)

    eye = (row == col).astype(F32)
    n_sq = int(math.log2(c))
    probs = [(bi, h) for bi in range(nb) for h in range(RW_HEADS)]
    n = range(len(probs))
    hsl = [slice(h * RW_DH, (h + 1) * RW_DH) for _, h in probs]
    bf = lambda x: x.astype(BF16)
    s_old = [s_ref[bi, h] for bi, h in probs]
    a_h = [a_hat[bi][:, hsl[i]] for i, (bi, _) in enumerate(probs)]
    v_h = [v[bi][:, hsl[i]] for i, (bi, _) in enumerate(probs)]
    r_h = [r_hat[bi][:, hsl[i]] for i, (bi, _) in enumerate(probs)]
    vb = [bf(x) for x in v_h]
    prod = [jnp.where(keep, _dot(bf(jnp.concatenate([a_h[i], r_h[i]], axis=0)),
                                 bf(jnp.concatenate([b_hat[bi][:, hsl[i]], k_hat[bi][:, hsl[i]]], axis=0)),
                                 ((1,), (1,))), 0.0) for i, (bi, _) in enumerate(probs)]
    p = [prod[i][:c, :c] for i in n]
    t = [eye + p[i] for i in n]
    w = [_dot(bf(prod[i][:c, c:]), vb[i]) for i in n]
    for _ in range(n_sq - 1):
        pb = [bf(p[i]) for i in n]
        p = [_dot(pb[i], pb[i]) for i in n]
        t = [t[i] + _dot(bf(p[i]), bf(t[i])) for i in n]
    tb = [bf(t[i]) for i in n]
    u0 = [_dot(tb[i], bf(w[i])) for i in n]
    ta = [_dot(tb[i], bf(a_h[i])) for i in n]
    xs = [_dot(bf(jnp.concatenate([ta[i], r_h[i]], axis=0)), bf(s_old[i]), ((1,), (1,))) for i in n]
    uv = [bf(jnp.concatenate([xs[i][:c] + u0[i], v_h[i]], axis=0)) for i in n]
    s_new = [s_old[i] * gamma[bi][:, hsl[i]]
             + _dot(uv[i], bf(jnp.concatenate([b_end[bi][:, hsl[i]], k_end[bi][:, hsl[i]]], axis=0)),
                    ((0,), (0,))) for i, (bi, _) in enumerate(probs)]
    y = [xs[i][c:] + _dot(bf(prod[i][c:, :]), uv[i]) for i in n]
    for i, (bi, h) in enumerate(probs):
        s_ref[bi, h] = s_new[i]
    for i, (bi, h) in enumerate(probs):
        sl = hsl[i]
        mu = jnp.mean(y[i], axis=-1, keepdims=True)
        yc = y[i] - mu
        var = jnp.mean(yc * yc, axis=-1, keepdims=True)
        yn = yc * lax.rsqrt(var + LNX_EPS) * lnw_ref[:, sl] + lnb_ref[:, sl]
        bonus = jnp.sum(rkk[bi][:, sl], axis=-1, keepdims=True) * v_h[i]
        out_ref[bi, :, sl] = (yn + bonus) * g_ref[bi, :, sl]


def _wkv(prep, s0, p):
    r = prep[0]
    bx, tx, _ = r.shape
    c = WKV_CHUNK
    nb = WKV_BATCH
    blk = pl.BlockSpec((nb, c, RW_W), lambda b, t: (b, t, 0))
    vec = pl.BlockSpec((1, RW_W), lambda b, t: (0, 0))
    st = pl.BlockSpec((nb, RW_HEADS, RW_DH, RW_DH), lambda b, t: (b, 0, 0, 0))
    return pl.pallas_call(
        _wkv_kernel,
        grid=(bx // nb, tx // c),
        in_specs=[blk] * 7 + [st, vec, vec, vec],
        out_specs=[blk, st],
        out_shape=[jax.ShapeDtypeStruct((bx, tx, RW_W), F32),
                   jax.ShapeDtypeStruct((bx, RW_HEADS, RW_DH, RW_DH), F32)],
        compiler_params=_cparams(("parallel", "arbitrary")),
        name="wkv_chunk",
    )(*prep, s0, p["r_k"], p["lnx_w"], p["lnx_b"])


def _attn_kernel(lam_init, lam_ref, q_ref, k_ref, v_ref, sw_ref, o_ref, kt_ref, vx_ref, m_ref, acc_ref):
    qi = pl.program_id(2)
    tq = q_ref.shape[0]

    @pl.when(qi == 0)
    def _():
        kt_ref[...] = k_ref[...].T.astype(BF16)
        vx_ref[:, :DF_DV] = v_ref[...].astype(BF16)
        vx_ref[:, DF_DV:] = jnp.ones((vx_ref.shape[0], DF_DV), BF16)

    q = (q_ref[...] * (DF_DK ** -0.5)).astype(BF16)
    qc = [q[:, :DF_DK], q[:, DF_DK:]]
    m_ref[...] = jnp.full(m_ref.shape, -jnp.inf, F32)
    acc_ref[...] = jnp.zeros(acc_ref.shape, F32)
    def block(start, width, masked):
        vblk = vx_ref[pl.ds(start, width), :]
        s = [_dot(qc[c], kt_ref[c * DF_DK:(c + 1) * DF_DK, pl.ds(start, width)]) for c in range(2)]
        if masked:
            rowi = lax.broadcasted_iota(jnp.int32, (tq, width), 0)
            coli = lax.broadcasted_iota(jnp.int32, (tq, width), 1)
            causal = coli <= rowi + (width - tq)
            s = [jnp.where(causal, s[c], -jnp.inf) for c in range(2)]
        m_old = [m_ref[c] for c in range(2)]
        m_new = [jnp.maximum(m_old[c], jnp.max(s[c], axis=-1, keepdims=True)) for c in range(2)]
        pr = [jnp.exp(s[c] - m_new[c]).astype(BF16) for c in range(2)]
        pv = [_dot(pr[c], vblk) for c in range(2)]
        for c in range(2):
            acc_ref[c] = jnp.exp(m_old[c] - m_new[c]) * acc_ref[c] + pv[c]
            m_ref[c] = m_new[c]

    def pair(pj, carry):
        block(pl.multiple_of(pj * (2 * tq), 2 * tq), 2 * tq, False)
        return carry

    lax.fori_loop(0, qi // 2, pair, 0)

    @pl.when(qi % 2 == 1)
    def _():
        block(pl.multiple_of((qi - 1) * tq, tq), 2 * tq, True)

    @pl.when(qi % 2 == 0)
    def _():
        block(pl.multiple_of(qi * tq, tq), tq, True)
    o = (acc_ref[0, :, :DF_DV] / acc_ref[0, :, DF_DV:]
         - lam_ref[0] * (acc_ref[1, :, :DF_DV] / acc_ref[1, :, DF_DV:]))
    o = o * lax.rsqrt(jnp.mean(o * o, axis=-1, keepdims=True) + SUBLN_EPS)
    o_ref[...] = o * sw_ref[...] * (1.0 - lam_init)


def _attn_prompt(q, k, v, b, tp, lam, subln_w, lam_init):
    tq = ATT_TILE
    nq = tp // tq
    return pl.pallas_call(
        functools.partial(_attn_kernel, lam_init),
        grid=(b, DF_HEADS, nq),
        in_specs=[pl.BlockSpec(memory_space=pltpu.SMEM),
                  pl.BlockSpec((tq, DF_DV), lambda bi, h, i: (bi * nq + i, h)),
                  pl.BlockSpec((tp, DF_DV), lambda bi, h, i: (bi, h)),
                  pl.BlockSpec((tp, DF_DV), lambda bi, h, i: (bi, h)),
                  pl.BlockSpec((1, DF_DV), lambda bi, h, i: (0, h))],
        out_specs=pl.BlockSpec((tq, DF_DV), lambda bi, h, i: (bi * nq + i, h)),
        out_shape=jax.ShapeDtypeStruct((b * tp, DF_W), F32),
        scratch_shapes=[pltpu.VMEM((2 * DF_DK, tp), BF16), pltpu.VMEM((tp, 2 * DF_DV), BF16),
                        pltpu.VMEM((2, tq, 1), F32), pltpu.VMEM((2, tq, 2 * DF_DV), F32)],
        compiler_params=_cparams(("parallel", "parallel", "arbitrary")),
        name="diff_attn_prompt",
    )(lam, q, k, v, subln_w)


def _decode_kernel(lam_init, n_pg, pt_ref, lam_ref, q_ref, kn_ref, vn_ref, *refs):
    ck_refs = refs[:n_pg]
    cv_refs = refs[n_pg:2 * n_pg]
    sw_ref, o_ref, qrow_ref, m_ref, l_ref, acc_ref = refs[2 * n_pg:]
    pi = pl.program_id(1)
    n_pages = pl.num_programs(1)
    nq = 2 * DF_HEADS
    page = ck_refs[0].shape[2]

    @pl.when(pi == 0)
    def _():
        rowi = lax.broadcasted_iota(jnp.int32, (nq, DF_W), 0)
        coli = lax.broadcasted_iota(jnp.int32, (nq, DF_W), 1)
        qb = jnp.broadcast_to(q_ref[0] * (DF_DK ** -0.5), (nq, DF_W))
        qrow_ref[...] = jnp.where(jnp.right_shift(coli, int(math.log2(DF_DK))) == rowi, qb, 0.0)
        m_ref[...] = jnp.full(m_ref.shape, -jnp.inf, F32)
        l_ref[...] = jnp.zeros(l_ref.shape, F32)
        acc_ref[...] = jnp.zeros(acc_ref.shape, F32)

    qrow = qrow_ref[...]
    qb = qrow.astype(BF16)
    s = jnp.concatenate([_dot(qb, ck_refs[g][0].astype(BF16)) for g in range(n_pg)], axis=1)
    m_old = m_ref[...]
    m_new = jnp.maximum(m_old, jnp.max(s, axis=-1, keepdims=True))
    alpha = jnp.exp(m_old - m_new)
    pr = jnp.exp(s - m_new)
    l_ref[...] = alpha * l_ref[...] + jnp.sum(pr, axis=-1, keepdims=True)
    prb = pr.astype(BF16)
    for h in range(DF_HEADS):
        pv = None
        for g in range(n_pg):
            v_gh = cv_refs[g][0, pl.ds(h, page, stride=DF_HEADS), :].astype(BF16)
            d = _dot(prb[:, g * page:(g + 1) * page], v_gh)
            pv = d if pv is None else pv + d
        sl = slice(h * DF_DV, (h + 1) * DF_DV)
        acc_ref[:, sl] = alpha * acc_ref[:, sl] + pv
    m_ref[...] = m_new

    @pl.when(pi == n_pages - 1)
    def _():
        s_self = jnp.sum(qrow * kn_ref[0], axis=-1, keepdims=True)
        m_old = m_ref[...]
        m_fin = jnp.maximum(m_old, s_self)
        alpha = jnp.exp(m_old - m_fin)
        p_self = jnp.exp(s_self - m_fin)
        l_fin = alpha * l_ref[...] + p_self
        acc = (alpha * acc_ref[...] + p_self * vn_ref[0]) / l_fin
        for h in range(DF_HEADS):
            sl = slice(h * DF_DV, (h + 1) * DF_DV)
            o = acc[2 * h:2 * h + 1, sl] - lam_ref[0] * acc[2 * h + 1:2 * h + 2, sl]
            o = o * lax.rsqrt(jnp.mean(o * o, axis=-1, keepdims=True) + SUBLN_EPS)
            o_ref[0, :, sl] = o * sw_ref[:, sl] * (1.0 - lam_init)


def _attn_sample(q, k_new, v_new, cache_k, cache_v, page_table, lam, subln_w, lam_init):
    bs = q.shape[0]
    n_pages = page_table.shape[1]
    n_pool, page = cache_k.shape[:2]
    ck = jnp.transpose(cache_k, (0, 2, 3, 4, 1)).reshape(n_pool, DF_W, page)
    cv = cache_v.reshape(n_pool, page * DF_HEADS, DF_DV)
    n_pg = math.gcd(DEC_PAGES, n_pages)
    row = pl.BlockSpec((1, 1, DF_W), lambda b, p, pt: (b, 0, 0))
    ck_specs = [pl.BlockSpec((1, DF_W, page), lambda b, p, pt, g=g: (pt[b, p * n_pg + g], 0, 0))
                for g in range(n_pg)]
    cv_specs = [pl.BlockSpec((1, page * DF_HEADS, DF_DV), lambda b, p, pt, g=g: (pt[b, p * n_pg + g], 0, 0))
                for g in range(n_pg)]
    nq = 2 * DF_HEADS
    grid_spec = pltpu.PrefetchScalarGridSpec(
        num_scalar_prefetch=1,
        grid=(bs, n_pages // n_pg),
        in_specs=[pl.BlockSpec(memory_space=pltpu.SMEM), row, row, row] + ck_specs + cv_specs
                 + [pl.BlockSpec((1, DF_W), lambda b, p, pt: (0, 0))],
        out_specs=row,
        scratch_shapes=[pltpu.VMEM((nq, DF_W), F32), pltpu.VMEM((nq, 1), F32),
                        pltpu.VMEM((nq, 1), F32), pltpu.VMEM((nq, DF_W), F32)],
    )
    out = pl.pallas_call(
        functools.partial(_decode_kernel, lam_init, n_pg),
        grid_spec=grid_spec,
        out_shape=jax.ShapeDtypeStruct((bs, 1, DF_W), F32),
        compiler_params=_cparams(("parallel", "arbitrary")),
        name="diff_attn_paged",
    )(page_table, lam, q.reshape(bs, 1, DF_W), k_new.reshape(bs, 1, DF_W), v_new.reshape(bs, 1, DF_W),
      *([ck] * n_pg), *([cv] * n_pg), subln_w)
    return out.reshape(bs, DF_W)


def _layer_norm(x, w, b):
    mu = jnp.mean(x, axis=-1, keepdims=True)
    xc = x - mu
    var = jnp.mean(xc * xc, axis=-1, keepdims=True)
    return xc * lax.rsqrt(var + LN_EPS) * w + b


def _outproj_kernel(n_main, x_ref, rw_ref, o_ref, rwt_ref, ot_ref, w_ref, lnw_ref, lnb_ref, y_ref, yt_ref):
    main = pl.program_id(0) < n_main
    rw = jnp.where(main, rw_ref[...], rwt_ref[...]).astype(BF16)
    o = jnp.where(main, o_ref[...], ot_ref[...]).astype(BF16)
    mix = _dot(rw, w_ref[:RW_W, :]) + _dot(o, w_ref[RW_W:, :])
    y = _layer_norm(DEEP_ALPHA * x_ref[...] + mix, lnw_ref[...], lnb_ref[...])
    y_ref[...] = y
    yt_ref[...] = y.T.astype(BF16)


def _outproj(x, rw_main, o_main, rw_tail, o_tail, w_bf, ln_w, ln_b):
    rows, d = x.shape
    tm = ROW_TILE
    n_main = rw_main.shape[0] // tm
    assert rw_main.shape[0] % tm == 0 and rows == (n_main + 1) * tm and rw_tail.shape[0] == tm
    row = lambda i: (i, 0)
    main = lambda i: (jnp.minimum(i, n_main - 1), 0)
    tail = lambda i: (0, 0)
    return pl.pallas_call(
        functools.partial(_outproj_kernel, n_main),
        grid=(rows // tm,),
        in_specs=[pl.BlockSpec((tm, d), row), pl.BlockSpec((tm, RW_W), main), pl.BlockSpec((tm, DF_W), main),
                  pl.BlockSpec((tm, RW_W), tail), pl.BlockSpec((tm, DF_W), tail),
                  pl.BlockSpec(w_bf.shape, lambda i: (0, 0)),
                  pl.BlockSpec((1, d), lambda i: (0, 0)), pl.BlockSpec((1, d), lambda i: (0, 0))],
        out_specs=[pl.BlockSpec((tm, d), row), pl.BlockSpec((d, tm), lambda i: (0, i))],
        out_shape=[jax.ShapeDtypeStruct((rows, d), F32), jax.ShapeDtypeStruct((d, rows), BF16)],
        compiler_params=_cparams(("parallel",)),
        name="outproj_ln",
    )(x, rw_main, o_main, rw_tail, o_tail, w_bf, ln_w, ln_b)


def _peer_pairs():
    n = PK_TOPK + 1
    return [(i, j) for i in range(n) for j in range(n) if (i + 1) * (j + 1) <= n]


def _sort_pairs(n):
    pairs = []
    p = 1
    while p < n:
        k = p
        while k >= 1:
            for j in range(k % p, n - k, 2 * k):
                for i in range(min(k, n - j - k)):
                    if (i + j) // (2 * p) == (i + j + k) // (2 * p):
                        pairs.append((i + j, i + j + k))
            k //= 2
        p *= 2
    return pairs


def _extract_top(cur, n, n_pad):
    rows, cols = cur.shape
    n_tiles = rows // SUBLANES
    size = 1 << (n_tiles - 1).bit_length()
    neg = jnp.full((SUBLANES, cols), -jnp.inf, F32)
    t = [cur[i * SUBLANES:(i + 1) * SUBLANES, :] for i in range(n_tiles)] + [neg] * (size - n_tiles)
    for i, j in _sort_pairs(size):
        t[i], t[j] = jnp.maximum(t[i], t[j]), jnp.minimum(t[i], t[j])
    vals = []
    rowi = lax.broadcasted_iota(jnp.int32, (n_pad, cols), 0)
    stacked = jnp.zeros((n_pad, cols), F32)
    for k in range(n):
        m = jnp.max(t[0], axis=0, keepdims=True)
        vals.append(m)
        stacked = jnp.where(rowi == k, m, stacked)
        hit = t[0] == m
        keep = min(len(t), n - 1 - k)
        t = [jnp.where(hit, t[l + 1] if l + 1 < len(t) else neg, t[l]) for l in range(keep)]
    return vals, stacked


def _peer_score_kernel(x_ref, wq_ref, sk_ref, sela_ref, selb_ref,
                       e1_ref, e2_ref, psi_ref, qt_ref):
    n_top = PK_TOPK + 1
    half = PK_DQ // 2
    tt = x_ref.shape[0]
    x = x_ref[...]
    x_hi = x.astype(BF16)
    x_lo = (x - x_hi.astype(F32)).astype(BF16)
    nt = ((1,), (1,))
    qt_ref[...] = (_dot(wq_ref[0], x_hi, nt)
                   + (_dot(wq_ref[1], x_hi, nt) + _dot(wq_ref[0], x_lo, nt)))
    n_cand = sela_ref.shape[0]
    n_pad = sela_ref.shape[1]
    cand_row = lax.broadcasted_iota(jnp.int32, (n_cand, tt), 0)
    n_pairs = len(_peer_pairs())
    for h in range(PK_HEADS):
        s1 = _dot3(sk_ref[h, 0], qt_ref[pl.ds(h * PK_DQ, half), :])
        s2 = _dot3(sk_ref[h, 1], qt_ref[pl.ds(h * PK_DQ + half, half), :])
        top1, a_st = _extract_top(s1, n_top, n_pad)
        top2, b_st = _extract_top(s2, n_top, n_pad)
        cand = _dot(sela_ref[...], a_st, precision=HI) + _dot(selb_ref[...], b_st, precision=HI)
        cand = jnp.where(cand_row < n_pairs, cand, -jnp.inf)
        ctop, _ = _extract_top(cand, n_top, n_pad)
        cmax = ctop[0]
        theta = 0.5 * (ctop[PK_TOPK - 1] + ctop[PK_TOPK])
        z = jnp.sum(jnp.where(cand >= theta, jnp.exp(cand - cmax), 0.0), axis=0, keepdims=True)
        e1_ref[h] = jnp.exp(s1 - top1[0]) * (0.5 / z)
        e2_ref[h] = jnp.exp(s2 - top2[0])
        psi_ref[h] = jnp.exp(theta - s1 - top2[0])


def _peer_scores(x1, wq_t, subkeys, sel_a, sel_b):
    rows, d = x1.shape
    tt = PEER_TOK
    out = jax.ShapeDtypeStruct((PK_HEADS, N_KEYS, rows), F32)
    ospec = pl.BlockSpec((PK_HEADS, N_KEYS, tt), lambda i: (0, 0, i))
    return pl.pallas_call(
        _peer_score_kernel,
        grid=(rows // tt,),
        in_specs=[pl.BlockSpec((tt, d), lambda i: (i, 0)),
                  pl.BlockSpec(wq_t.shape, lambda i: (0, 0, 0)),
                  pl.BlockSpec(subkeys.shape, lambda i: (0, 0, 0, 0)),
                  pl.BlockSpec(sel_a.shape, lambda i: (0, 0)),
                  pl.BlockSpec(sel_b.shape, lambda i: (0, 0))],
        out_specs=[ospec] * 3,
        out_shape=[out] * 3,
        scratch_shapes=[pltpu.VMEM((PK_HEADS * PK_DQ, tt), F32)],
        compiler_params=_cparams(("parallel",)),
        name="peer_scores",
    )(x1, wq_t, subkeys, sel_a, sel_b)


def _peer_expert_kernel(x_ref, xt_ref, u_ref, vt_ref, e1_ref, e2_ref, psi_ref, lnw_ref, lnb_ref,
                        y_ref, wt_ref, acc_ref, act_ref, pb_ref, eb_ref):
    ei = pl.program_id(1)
    te = u_ref.shape[0]
    tt = x_ref.shape[0]

    @pl.when(ei == 0)
    def _():
        acc_ref[...] = jnp.zeros(acc_ref.shape, F32)

    n_rows = te // N_KEYS
    k_rows = PEER_KSPLIT // N_KEYS
    n_sub = N_KEYS // SUBLANES
    for j in range(n_rows):
        i1 = ei * n_rows + j
        for h in range(PK_HEADS):
            pb_ref[j, h] = jnp.broadcast_to(psi_ref[h, pl.ds(i1, 1), :], (SUBLANES, tt))
            eb_ref[j, h] = jnp.broadcast_to(e1_ref[h, pl.ds(i1, 1), :], (SUBLANES, tt))
    n_grp = n_rows // k_rows

    def first_product(g):
        act_ref[g % 2] = _dot(u_ref[g * PEER_KSPLIT:(g + 1) * PEER_KSPLIT, :], xt_ref[...])

    def second_product(g):
        ks = slice(g * PEER_KSPLIT, (g + 1) * PEER_KSPLIT)
        return _dot(vt_ref[:, ks], wt_ref[ks, :])

    def gate_row(j):
        rows = slice(j * N_KEYS, (j + 1) * N_KEYS)
        arow = slice((j % k_rows) * N_KEYS, (j % k_rows + 1) * N_KEYS)
        for tg in range(tt // LANES):
            ls = slice(tg * LANES, (tg + 1) * LANES)
            gate = jnp.zeros((n_sub, SUBLANES, LANES), F32)
            for h in range(PK_HEADS):
                e2 = e2_ref[h, :, ls].reshape(n_sub, SUBLANES, LANES)
                gate = gate + jnp.where(e2 >= pb_ref[j, h, :, ls][None], eb_ref[j, h, :, ls][None] * e2, 0.0)
            a = act_ref[(j // k_rows) % 2, arow, ls]
            gate = gate.reshape(N_KEYS, LANES)
            wt_ref[rows, ls] = (gate * (a + a * lax.erf(a * (2.0 ** -0.5)))).astype(BF16)

    first_product(0)
    part = None
    for g in range(n_grp):
        if g + 1 < n_grp:
            first_product(g + 1)
        for r in range(k_rows):
            gate_row(g * k_rows + r)
            if r == k_rows // 2 - 1 and g > 0:
                d = second_product(g - 1)
                part = d if part is None else part + d
    d = second_product(n_grp - 1)
    acc_ref[...] += d if part is None else part + d

    @pl.when(ei == pl.num_programs(1) - 1)
    def _():
        ch = acc_ref[...].T
        y_ref[...] = _layer_norm(DEEP_ALPHA * x_ref[...] + ch, lnw_ref[...], lnb_ref[...])


def _peer_experts(x1, x1_t, u_bf, vt_bf, e1, e2, psi, ln_w, ln_b):
    rows, d = x1.shape
    n_exp = u_bf.shape[0]
    tt = PEER_TOK
    te = PEER_EXP
    tok = lambda i, e: (i, 0)
    sc = pl.BlockSpec((PK_HEADS, N_KEYS, tt), lambda i, e: (0, 0, i))
    vec = pl.BlockSpec((1, d), lambda i, e: (0, 0))
    return pl.pallas_call(
        _peer_expert_kernel,
        grid=(rows // tt, n_exp // te),
        in_specs=[pl.BlockSpec((tt, d), tok), pl.BlockSpec((d, tt), lambda i, e: (0, i)),
                  pl.BlockSpec((te, d), lambda i, e: (e, 0)),
                  pl.BlockSpec((d, te), lambda i, e: (0, e)),
                  sc, sc, sc, vec, vec],
        out_specs=pl.BlockSpec((tt, d), tok),
        out_shape=jax.ShapeDtypeStruct((rows, d), F32),
        scratch_shapes=[pltpu.VMEM((te, tt), BF16), pltpu.VMEM((d, tt), F32),
                        pltpu.VMEM((2, PEER_KSPLIT, tt), F32),
                        pltpu.VMEM((te // N_KEYS, PK_HEADS, SUBLANES, tt), F32),
                        pltpu.VMEM((te // N_KEYS, PK_HEADS, SUBLANES, tt), F32)],
        compiler_params=_cparams(("parallel", "arbitrary")),
        name="peer_experts",
    )(x1, x1_t, u_bf, vt_bf, e1, e2, psi, ln_w, ln_b)


def _rope_tables(pos):
    half = DF_DK // 2
    inv = ROPE_THETA ** (-jnp.arange(half, dtype=F32) / half)
    ang = pos.astype(F32)[:, None] * inv[None, :]
    cos, sin = jnp.cos(ang), jnp.sin(ang)
    cos128 = jnp.concatenate([cos, cos, cos, cos], axis=1)
    sin128 = jnp.concatenate([-sin, sin, -sin, sin], axis=1)
    return cos128, sin128


def _round_up(n, m):
    return -(-n // m) * m


def kernel(x_prompt, x_sample, cache_k, cache_v, page_table, state_wkv, state_shift, meta_tokens, w_in,
           mu_shift, w0, w2_decay, a0, a2, g2, k_k, k_a, r_k, lnx_w, lnx_b, lam_q1, lam_k1, lam_q2, lam_k2,
           subln_w, w_out, ln1_w, ln1_b, pk_wq, pk_subkeys, peer_u, peer_v, ln2_w, ln2_b):
    assert w_in.shape[0] == DEPTH == 1
    bp, seq, d = x_prompt.shape
    bs, dec_seq, _ = x_sample.shape
    assert dec_seq == 1
    t_prompt = N_META + seq
    tp = _round_up(t_prompt, math.lcm(ATT_TILE, PREP_TILE, WKV_CHUNK))
    n_prompt_rows = bp * tp
    rows = _round_up(n_prompt_rows + bs, ROW_TILE)
    page = cache_k.shape[2]
    past = page_table.shape[1] * page
    lam_init = 0.8 - 0.6 * math.exp(-0.3 * 0)
    row2 = lambda a: a.reshape(1, -1)

    meta = meta_tokens.astype(x_prompt.dtype)
    pad = jnp.zeros((tp - t_prompt, d), x_prompt.dtype)
    pieces = [p for b in range(bp) for p in (meta, x_prompt[b], pad)]
    x_all = jnp.concatenate(pieces + [x_sample.reshape(bs, d),
                                      jnp.zeros((rows - n_prompt_rows - bs, d), x_prompt.dtype)], axis=0)
    pos = jnp.concatenate([jnp.tile(jnp.arange(tp), bp), jnp.full((rows - n_prompt_rows,), past)])
    cos, sin = _rope_tables(pos)

    rw_cols, q, k, v = _inproj(x_all, w_in[0].astype(BF16), cos, sin)

    sm = lambda a: a[n_prompt_rows:n_prompt_rows + bs]

    hs = jnp.arange(RW_W) // RW_DH
    rw_params = {
        "mu_shift": row2(mu_shift[0]), "w0": row2(w0[0]), "w2_decay": w2_decay[0], "a0": row2(a0[0]),
        "a2": a2[0], "g2": g2[0], "k_k": row2(k_k[0]), "k_a": row2(k_a[0]),
        "headsum": (hs[:, None] == hs[None, :]).astype(F32),
        "r_k": row2(r_k[0]), "lnx_w": row2(lnx_w[0]), "lnx_b": row2(lnx_b[0]),
    }
    prep_p = _rwkv_prep(rw_cols, bp, tp, jnp.zeros((bp, RW_COLS), F32), t_prompt, rw_params)
    rw_out_p, wkv_p = _wkv(prep_p, jnp.zeros((bp, RW_HEADS, RW_DH, RW_DH), F32), rw_params)
    ts = WKV_CHUNK
    rw_s = jnp.concatenate([sm(rw_cols)[:, None, :], jnp.zeros((bs, ts - 1, RW_COLS), F32)], axis=1)
    prep_s = _rwkv_prep(rw_s.reshape(bs * ts, RW_COLS), bs, ts, state_shift[0], dec_seq, rw_params)
    rw_out_s, wkv_s = _wkv(prep_s, state_wkv[0], rw_params)

    f32 = lambda t: t.astype(F32)
    lam = (jnp.exp(jnp.sum(f32(lam_q1[0]) * f32(lam_k1[0])))
           - jnp.exp(jnp.sum(f32(lam_q2[0]) * f32(lam_k2[0]))) + lam_init).reshape(1)
    sw = row2(subln_w[0])
    o_p = _attn_prompt(q, k, v, bp, tp, lam, sw, lam_init)
    o_s = _attn_sample(sm(q), sm(k), sm(v), cache_k[0], cache_v[0], page_table, lam, sw, lam_init)

    tail = jnp.zeros((rows - n_prompt_rows - bs, RW_W), F32)
    x1, x1_t = _outproj(x_all, rw_out_p.reshape(n_prompt_rows, RW_W), o_p,
                        jnp.concatenate([rw_out_s[:, 0, :], tail], axis=0), jnp.concatenate([o_s, tail], axis=0),
                        w_out[0].astype(BF16), row2(ln1_w[0]), row2(ln1_b[0]))

    pairs = _peer_pairs()
    n_cand = _round_up(len(pairs), SUBLANES)
    n_pad = _round_up(PK_TOPK + 1, SUBLANES)
    ia = jnp.array([p[0] for p in pairs] + [0] * (n_cand - len(pairs)))
    ib = jnp.array([p[1] for p in pairs] + [0] * (n_cand - len(pairs)))
    sel_a = (ia[:, None] == jnp.arange(n_pad)[None, :]).astype(F32)
    sel_b = (ib[:, None] == jnp.arange(n_pad)[None, :]).astype(F32)
    wq_t = pk_wq[0].T.astype(F32)
    wq_hi = wq_t.astype(BF16)
    wq_lo = (wq_t - wq_hi.astype(F32)).astype(BF16)
    e1, e2, psi = _peer_scores(x1, jnp.stack([wq_hi, wq_lo]), pk_subkeys[0], sel_a, sel_b)
    y = _peer_experts(x1, x1_t, peer_u[0].astype(BF16), peer_v[0].T.astype(BF16),
                      e1, e2, psi, row2(ln2_w[0]), row2(ln2_b[0]))

    seq = lambda a, lo, hi: jnp.stack([a[b * tp + lo:b * tp + hi] for b in range(bp)])
    y_prompt = seq(y, N_META, t_prompt)
    y_sample = sm(y).reshape(bs, dec_seq, d)
    k_p = seq(k, 0, t_prompt).reshape(1, bp, t_prompt, DF_HEADS, 2, DF_DK)
    v_p = seq(v, 0, t_prompt).reshape(1, bp, t_prompt, DF_HEADS, DF_DV)
    shift_p = seq(rw_cols, t_prompt - 1, t_prompt).reshape(1, bp, RW_COLS)
    k_s = sm(k).reshape(1, bs, dec_seq, DF_HEADS, 2, DF_DK)
    v_s = sm(v).reshape(1, bs, dec_seq, DF_HEADS, DF_DV)
    shift_s = sm(rw_cols)[None]
    return (y_prompt, y_sample, k_p, v_p, wkv_p[None], shift_p, k_s, v_s, wkv_s[None], shift_s)
```
